```python
import jax, jax.numpy as jnp
from jax import lax
import numpy as np

D_MODEL = 1024
BATCH = 32
SEQ = 2048
DEPTH = 1

CONV_WIDTH = 512
CONV_GROUPS = 8
CONV_K = 3
SB_HEADS = 8
SB_HEAD_DIM = 64
SB_WIDTH = SB_HEADS * SB_HEAD_DIM
MIX_WIDTH = CONV_WIDTH + SB_WIDTH
PROJ_WIDTH = 3 * CONV_WIDTH + 3 * SB_WIDTH
SPLITS = (CONV_WIDTH, 2 * CONV_WIDTH, 3 * CONV_WIDTH,
          3 * CONV_WIDTH + SB_WIDTH, 3 * CONV_WIDTH + 2 * SB_WIDTH)
Q_BLOCK = 128
MEM_LEN = 256
MEM_HEADS = 4
MEM_HEAD_DIM = D_MODEL // MEM_HEADS
N_GROUPS = 4
EXPERTS_PER_GROUP = 8
N_EXPERTS = N_GROUPS * EXPERTS_PER_GROUP
TOP_K = 2
D_EXPERT = D_MODEL // 2
DISPATCH_BLOCK = 256
DEEPNORM_ALPHA = (2 * DEPTH) ** 0.25
DEEPNORM_BETA = (8 * DEPTH) ** -0.25
LN_EPS = 1e-5
RMS_EPS = 1e-6

kernel_name = 'hybrid_conv_stickbreak_memxattn_hmoe_deepnorm'


def layer_norm(x, g, b):
    xf = x.astype(jnp.float32)
    mu = jnp.mean(xf, axis=-1, keepdims=True)
    var = jnp.mean(jnp.square(xf - mu), axis=-1, keepdims=True)
    y = (xf - mu) * lax.rsqrt(var + LN_EPS) * g.astype(jnp.float32) + b.astype(jnp.float32)
    return y.astype(x.dtype)


def group_rms_norm(y, n_groups, gain):
    bn, s, w = y.shape
    yf = y.astype(jnp.float32).reshape(bn, s, n_groups, w // n_groups)
    yf = yf * lax.rsqrt(jnp.mean(jnp.square(yf), axis=-1, keepdims=True) + RMS_EPS)
    return (yf.reshape(bn, s, w) * gain.astype(jnp.float32)).astype(y.dtype)


def causal_depthwise_conv(u, w):
    c = u.shape[-1]
    return lax.conv_general_dilated(
        u, w.astype(u.dtype)[:, None, :], window_strides=(1,),
        padding=[(CONV_K - 1, 0)], dimension_numbers=('NWC', 'WIO', 'NWC'),
        feature_group_count=c)


def stick_breaking_attention(q, k, v):
    s_len = q.shape[1]
    scale = SB_HEAD_DIM ** -0.5
    outs = []
    for start in range(0, s_len, Q_BLOCK):
        end = start + Q_BLOCK
        z = jnp.einsum('bqhd,bkhd->bhqk', q[:, start:end], k[:, :end]).astype(jnp.float32) * scale
        t_pos = start + jnp.arange(Q_BLOCK)[:, None]
        s_pos = jnp.arange(end)[None, :]
        before = s_pos < t_pos
        log1m = jnp.where(before, jax.nn.log_sigmoid(-z), 0.0)
        between = lax.cumsum(log1m, axis=3, reverse=True) - log1m
        a = jnp.where(before, jnp.exp(jax.nn.log_sigmoid(z) + between), 0.0)
        outs.append(jnp.einsum('bhqk,bkhd->bqhd', a.astype(v.dtype), v[:, :end]))
    return jnp.concatenate(outs, axis=1)


def hybrid_mixer(x, w_in, conv_w, gn_conv, gn_sb, w_out):
    bn, s, _ = x.shape
    proj = x @ w_in
    cb, cc, ch, q, k, v = jnp.split(proj, SPLITS, axis=-1)
    y_conv = cb * causal_depthwise_conv(cc * ch, conv_w)
    y_conv = group_rms_norm(y_conv, CONV_GROUPS, gn_conv)
    heads = lambda t: t.reshape(bn, s, SB_HEADS, SB_HEAD_DIM)
    y_sb = stick_breaking_attention(heads(q), heads(k), heads(v)).reshape(bn, s, SB_WIDTH)
    y_sb = group_rms_norm(y_sb, SB_HEADS, gn_sb)
    return jnp.concatenate([y_conv, y_sb], axis=-1) @ w_out


def memory_cross_attention(x, mem, w_q, w_k, w_v, w_o):
    bn, s, d = x.shape
    q = (x @ w_q).reshape(bn, s, MEM_HEADS, MEM_HEAD_DIM)
    k = (mem @ w_k).reshape(bn, MEM_LEN, MEM_HEADS, MEM_HEAD_DIM)
    v = (mem @ w_v).reshape(bn, MEM_LEN, MEM_HEADS, MEM_HEAD_DIM)
    sc = jnp.einsum('bqhd,bmhd->bhqm', q, k).astype(jnp.float32) * (MEM_HEAD_DIM ** -0.5)
    p = jax.nn.softmax(sc, axis=-1).astype(v.dtype)
    o = jnp.einsum('bhqm,bmhd->bqhd', p, v).reshape(bn, s, d)
    return o @ w_o


def routed_experts(xf, expert_id, gate, w_gate, w_up, w_down):
    n, d = xf.shape
    n_assign = n * TOP_K
    flat_e = expert_id.reshape(n_assign)
    order = jnp.argsort(flat_e)
    sorted_e = flat_e[order]
    tok = order // TOP_K
    counts = jnp.bincount(flat_e, length=N_EXPERTS)
    padded = (counts + DISPATCH_BLOCK - 1) // DISPATCH_BLOCK * DISPATCH_BLOCK
    padded_end = jnp.cumsum(padded)
    padded_start = padded_end - padded
    start = jnp.cumsum(counts) - counts
    dest = padded_start[sorted_e] + jnp.arange(n_assign) - start[sorted_e]
    n_blocks = -(-n_assign // DISPATCH_BLOCK) + N_EXPERTS
    block_expert = jnp.minimum(
        jnp.searchsorted(padded_end, jnp.arange(n_blocks) * DISPATCH_BLOCK, side='right'),
        N_EXPERTS - 1)
    xs = jnp.zeros((n_blocks * DISPATCH_BLOCK, d), xf.dtype).at[dest].set(xf[tok])

    def expert_block(args):
        xb, e = args
        h = jax.nn.silu(xb @ w_gate[e]) * (xb @ w_up[e])
        return h @ w_down[e]

    ys = lax.map(expert_block, (xs.reshape(n_blocks, DISPATCH_BLOCK, d), block_expert))
    ys = ys.reshape(-1, d)[dest] * gate.reshape(n_assign)[order][:, None].astype(xf.dtype)
    return jnp.zeros_like(xf).at[tok].add(ys)


def hierarchical_moe(x, w_rg, b_rg, w_re, b_re, w_gate, w_up, w_down):
    bn, s, d = x.shape
    xf = x.reshape(-1, d)
    n = xf.shape[0]
    g_logits = (xf @ w_rg).astype(jnp.float32) + b_rg.astype(jnp.float32)
    g_prob = jax.nn.softmax(g_logits, axis=-1)
    grp = jnp.argmax(g_logits, axis=-1)
    e_logits = ((xf @ w_re).astype(jnp.float32) + b_re.astype(jnp.float32)).reshape(
        n, N_GROUPS, EXPERTS_PER_GROUP)
    e_in_grp = jnp.take_along_axis(e_logits, grp[:, None, None], axis=1)[:, 0]
    top_logit, top_idx = lax.top_k(e_in_grp, TOP_K)
    gate = jnp.take_along_axis(g_prob, grp[:, None], axis=1) * jax.nn.softmax(top_logit, axis=-1)
    expert_id = grp[:, None] * EXPERTS_PER_GROUP + top_idx
    return routed_experts(xf, expert_id, gate, w_gate, w_up, w_down).reshape(bn, s, d)


def setup_inputs(seed: int = 0) -> dict:
    key = jax.random.key(seed)
    ks = jax.random.split(key, 26)
    L = DEPTH
    nrm = lambda k, shape, sc: jax.random.normal(k, shape, jnp.float32) * sc
    gain = lambda k, shape: 1.0 + 0.02 * jax.random.normal(k, shape, jnp.float32)
    bias = lambda k, shape: 0.02 * jax.random.normal(k, shape, jnp.float32)
    in_col_scale = jnp.asarray(np.concatenate([
        np.ones(PROJ_WIDTH - SB_WIDTH, np.float32),
        np.full(SB_WIDTH, DEEPNORM_BETA, np.float32)]))
    return {
        'x': nrm(ks[0], (BATCH, SEQ, D_MODEL), 1.0),
        'mem': nrm(ks[1], (BATCH, MEM_LEN, D_MODEL), 1.0),
        'w_in': nrm(ks[2], (L, D_MODEL, PROJ_WIDTH), D_MODEL ** -0.5) * in_col_scale,
        'conv_w': nrm(ks[3], (L, CONV_K, CONV_WIDTH), CONV_K ** -0.5),
        'gn_conv': gain(ks[4], (L, CONV_WIDTH)),
        'gn_sb': gain(ks[5], (L, SB_WIDTH)),
        'w_out': nrm(ks[6], (L, MIX_WIDTH, D_MODEL), MIX_WIDTH ** -0.5 * DEEPNORM_BETA),
        'ln1_g': gain(ks[7], (L, D_MODEL)),
        'ln1_b': bias(ks[8], (L, D_MODEL)),
        'w_mq': nrm(ks[9], (L, D_MODEL, D_MODEL), D_MODEL ** -0.5),
        'w_mk': nrm(ks[10], (L, D_MODEL, D_MODEL), D_MODEL ** -0.5),
        'w_mv': nrm(ks[11], (L, D_MODEL, D_MODEL), D_MODEL ** -0.5 * DEEPNORM_BETA),
        'w_mo': nrm(ks[12], (L, D_MODEL, D_MODEL), D_MODEL ** -0.5 * DEEPNORM_BETA),
        'ln2_g': gain(ks[13], (L, D_MODEL)),
        'ln2_b': bias(ks[14], (L, D_MODEL)),
        'w_router_group': nrm(ks[15], (L, D_MODEL, N_GROUPS), D_MODEL ** -0.5),
        'b_router_group': nrm(ks[16], (L, N_GROUPS), 0.01),
        'w_router_expert': nrm(ks[17], (L, D_MODEL, N_EXPERTS), D_MODEL ** -0.5),
        'b_router_expert': nrm(ks[18], (L, N_EXPERTS), 0.01),
        'w_gate': nrm(ks[19], (L, N_EXPERTS, D_MODEL, D_EXPERT), D_MODEL ** -0.5),
        'w_up': nrm(ks[20], (L, N_EXPERTS, D_MODEL, D_EXPERT), D_MODEL ** -0.5),
        'w_down': nrm(ks[21], (L, N_EXPERTS, D_EXPERT, D_MODEL), D_EXPERT ** -0.5 * DEEPNORM_BETA),
        'ln3_g': gain(ks[22], (L, D_MODEL)),
        'ln3_b': bias(ks[23], (L, D_MODEL)),
    }


def reference(x, mem, w_in, conv_w, gn_conv, gn_sb, w_out, ln1_g, ln1_b,
              w_mq, w_mk, w_mv, w_mo, ln2_g, ln2_b,
              w_router_group, b_router_group, w_router_expert, b_router_expert,
              w_gate, w_up, w_down, ln3_g, ln3_b):
    for l in range(DEPTH):
        h = hybrid_mixer(x, w_in[l], conv_w[l], gn_conv[l], gn_sb[l], w_out[l])
        x = layer_norm(DEEPNORM_ALPHA * x + h, ln1_g[l], ln1_b[l])
        h = memory_cross_attention(x, mem, w_mq[l], w_mk[l], w_mv[l], w_mo[l])
        x = layer_norm(DEEPNORM_ALPHA * x + h, ln2_g[l], ln2_b[l])
        h = hierarchical_moe(x, w_router_group[l], b_router_group[l],
                             w_router_expert[l], b_router_expert[l],
                             w_gate[l], w_up[l], w_down[l])
        x = layer_norm(DEEPNORM_ALPHA * x + h, ln3_g[l], ln3_b[l])
    return x
```

```python
import functools

import jax
import jax.numpy as jnp
import numpy as np
from jax import lax
from jax.experimental import pallas as pl
from jax.experimental.pallas import tpu as pltpu

CONV_WIDTH = 512
CONV_GROUP = 64
CONV_K = 3
SB_HEADS = 8
SB_HEAD_DIM = 64
SB_WIDTH = SB_HEADS * SB_HEAD_DIM
MEM_HEADS = 4
N_GROUPS = 4
EXPERTS_PER_GROUP = 8
N_EXPERTS = N_GROUPS * EXPERTS_PER_GROUP
TOP_K = 2
DEPTH = 1
ALPHA = (2 * DEPTH) ** 0.25
LN_EPS = 1e-5
RMS_EPS = 1e-6

LANES = 128
SUBLANES = 8
VMEM_LIMIT = 56 * 1024 * 1024

BF16 = jnp.bfloat16
F32 = jnp.float32


def _split_bf16(a):
    hi = a.astype(BF16)
    lo = (a - hi.astype(F32)).astype(BF16)
    return hi, lo


def _dot(a, b):
    return jnp.dot(a, b, preferred_element_type=F32)


def _dot_nt(a, b):
    return lax.dot_general(a, b, (((1,), (1,)), ((), ())), preferred_element_type=F32)


def _layer_norm(r, g, b):
    mu = jnp.mean(r, axis=-1, keepdims=True)
    c = r - mu
    var = jnp.mean(c * c, axis=-1, keepdims=True)
    return c * lax.rsqrt(var + LN_EPS) * g + b


def _proj_conv_kernel(x_ref, w_ref, cw_ref, gn_ref, gmat_ref,
                      yc_ref, q_ref, k_ref, v_ref, ubuf_ref, *, ts):
    s = pl.program_id(1)
    xb = x_ref[0].astype(BF16)
    cw = CONV_WIDTH

    @pl.when(s == 0)
    def _():
        ubuf_ref[0:SUBLANES, :] = jnp.zeros((SUBLANES, cw), F32)

    cc = _dot(xb, w_ref[:, cw:2 * cw])
    ch = _dot(xb, w_ref[:, 2 * cw:3 * cw])
    u = cc * ch
    ubuf_ref[SUBLANES:SUBLANES + ts, :] = u
    um1 = ubuf_ref[SUBLANES - 1:SUBLANES - 1 + ts, :]
    um2 = ubuf_ref[SUBLANES - 2:SUBLANES - 2 + ts, :]
    conv = cw_ref[2:3, :] * u + cw_ref[1:2, :] * um1 + cw_ref[0:1, :] * um2
    ubuf_ref[0:SUBLANES, :] = ubuf_ref[ts:ts + SUBLANES, :]

    cb = _dot(xb, w_ref[:, 0:cw])
    y = cb * conv
    hi, lo = _split_bf16(y * y)
    gsum = _dot(hi, gmat_ref[...]) + _dot(lo, gmat_ref[...])
    yn = y * lax.rsqrt(gsum * (1.0 / CONV_GROUP) + RMS_EPS) * gn_ref[...]
    yc_ref[0] = yn.astype(yc_ref.dtype)

    sw = SB_WIDTH
    base = 3 * cw
    q_ref[0] = (_dot(xb, w_ref[:, base:base + sw]) * (SB_HEAD_DIM ** -0.5)).astype(q_ref.dtype)
    k_ref[0] = _dot(xb, w_ref[:, base + sw:base + 2 * sw]).astype(k_ref.dtype)
    v_ref[0] = _dot(xb, w_ref[:, base + 2 * sw:base + 3 * sw]).astype(v_ref.dtype)


def _proj_conv(x, w_in_b, conv_w, gn_conv, ts):
    bn, sl, d = x.shape
    pw = w_in_b.shape[1]
    gidx = np.arange(CONV_WIDTH) // CONV_GROUP
    gmat = jnp.asarray(gidx[:, None] == gidx[None, :], BF16)
    outs = [jax.ShapeDtypeStruct((bn, sl, CONV_WIDTH), BF16)] * 4
    tile = lambda w: pl.BlockSpec((1, ts, w), lambda b, s: (b, s, 0))
    full = lambda a: pl.BlockSpec(a.shape, lambda b, s: (0,) * a.ndim)
    gn = gn_conv.reshape(1, CONV_WIDTH)
    return pl.pallas_call(
        functools.partial(_proj_conv_kernel, ts=ts),
        grid=(bn, sl // ts),
        in_specs=[tile(d), full(w_in_b), full(conv_w), full(gn), full(gmat)],
        out_specs=[tile(CONV_WIDTH)] * 4,
        out_shape=outs,
        scratch_shapes=[pltpu.VMEM((ts + 2 * SUBLANES, CONV_WIDTH), F32)],
        compiler_params=pltpu.CompilerParams(
            dimension_semantics=("arbitrary", "arbitrary"), vmem_limit_bytes=VMEM_LIMIT),
        name="proj_conv",
    )(x, w_in_b, conv_w, gn, gmat)


SB_TK = 128


def _softplus(z):
    return jnp.maximum(z, 0.0) + jnp.log(1.0 + jnp.exp(jnp.minimum(z, -z)))


def _sb_kernel(q_ref, k_ref, v_ref, gn_ref, cum_ref, o_ref, acc_ref, run_ref, *, tq):
    qi = pl.program_id(2)
    tk = SB_TK
    nd = tq // tk
    lane = lax.broadcasted_iota(jnp.int32, (1, LANES), 1)
    q_all = q_ref[0]
    cum = cum_ref[...]

    def tile(h, row0, kstart, diag):
        rows = tq - row0
        qm = jnp.where((lane // SB_HEAD_DIM) == h, q_all[row0:, :], jnp.zeros((), BF16))
        kb = k_ref[0, pl.ds(kstart, tk), :]
        vb = v_ref[0, pl.ds(kstart, tk), :]
        z = _dot_nt(qm, kb)
        sp = _softplus(z)
        if diag:
            t_loc = lax.broadcasted_iota(jnp.int32, (rows, tk), 0)
            s_loc = lax.broadcasted_iota(jnp.int32, (rows, tk), 1)
            before = s_loc < t_loc
            sp = jnp.where(before, sp, 0.0)
        hi, lo = _split_bf16(sp)
        cs = _dot(hi, cum) + _dot(lo, cum)
        run = run_ref[h, row0:, :]
        a = jnp.exp(z - cs[:, :tk] - run)
        if diag:
            a = jnp.where(before, a, 0.0)
        acc_ref[h, row0:, :] += _dot(a.astype(BF16), vb)
        run_ref[h, row0:, :] = run + cs[:, tk:]

    for h in range(2):
        acc_ref[h] = jnp.zeros((tq, LANES), F32)
        run_ref[h] = jnp.zeros((tq, LANES), F32)
        for d in reversed(range(nd)):
            tile(h, d * tk, pl.multiple_of(qi * tq + d * tk, tk), True)

        def body(j, carry, h=h):
            kstart = pl.multiple_of((qi * nd - 1 - j) * tk, tk)
            tile(h, 0, kstart, False)
            return carry

        lax.fori_loop(0, qi * nd, body, 0)

    y = jnp.where(lane < SB_HEAD_DIM, acc_ref[0], acc_ref[1])
    y2 = y * y
    s0 = jnp.sum(jnp.where(lane < SB_HEAD_DIM, y2, 0.0), axis=-1, keepdims=True)
    s1 = jnp.sum(jnp.where(lane < SB_HEAD_DIM, 0.0, y2), axis=-1, keepdims=True)
    ms = jnp.where(lane < SB_HEAD_DIM, s0, s1) * (1.0 / SB_HEAD_DIM)
    o_ref[0] = (y * lax.rsqrt(ms + RMS_EPS) * gn_ref[...]).astype(o_ref.dtype)


def _sb_attention(q, k, v, gn_sb, tq):
    bn, sl, w = q.shape
    npair = w // LANES
    j = np.arange(SB_TK)
    cum = np.concatenate([(j[:, None] >= j[None, :]), np.ones((SB_TK, SB_TK), bool)], axis=1)
    cum = jnp.asarray(cum, BF16)
    gn = gn_sb.reshape(1, w)
    return pl.pallas_call(
        functools.partial(_sb_kernel, tq=tq),
        grid=(bn, npair, sl // tq),
        in_specs=[
            pl.BlockSpec((1, tq, LANES), lambda b, p, i: (b, i, p)),
            pl.BlockSpec((1, sl, LANES), lambda b, p, i: (b, 0, p)),
            pl.BlockSpec((1, sl, LANES), lambda b, p, i: (b, 0, p)),
            pl.BlockSpec((1, LANES), lambda b, p, i: (0, p)),
            pl.BlockSpec(cum.shape, lambda b, p, i: (0, 0)),
        ],
        out_specs=pl.BlockSpec((1, tq, LANES), lambda b, p, i: (b, i, p)),
        out_shape=jax.ShapeDtypeStruct((bn, sl, w), BF16),
        scratch_shapes=[pltpu.VMEM((2, tq, LANES), F32), pltpu.VMEM((2, tq, LANES), F32)],
        compiler_params=pltpu.CompilerParams(
            dimension_semantics=("arbitrary", "arbitrary", "arbitrary"),
            vmem_limit_bytes=VMEM_LIMIT),
        name="sb_attention",
    )(q, k, v, gn, cum)


def _kv_kernel(mem_ref, wk_ref, wv_ref, k_ref, v_ref):
    nb, ml, d = mem_ref.shape
    m = mem_ref[...].reshape(nb * ml, d).astype(BF16)
    k_ref[...] = _dot(m, wk_ref[...]).reshape(nb, ml, d).astype(k_ref.dtype)
    v_ref[...] = _dot(m, wv_ref[...]).reshape(nb, ml, d).astype(v_ref.dtype)


def _kv_proj(mem, wk_b, wv_b, nb):
    bn, ml, d = mem.shape
    blk = pl.BlockSpec((nb, ml, d), lambda i: (i, 0, 0))
    full = pl.BlockSpec((d, d), lambda i: (0, 0))
    return pl.pallas_call(
        _kv_kernel,
        grid=(bn // nb,),
        in_specs=[blk, full, full],
        out_specs=[blk, blk],
        out_shape=[jax.ShapeDtypeStruct((bn, ml, d), BF16)] * 2,
        compiler_params=pltpu.CompilerParams(
            dimension_semantics=("arbitrary",), vmem_limit_bytes=VMEM_LIMIT),
        name="kv_proj",
    )(mem, wk_b, wv_b)


ROUTE_ROWS = 8
ROUTER_ROWS = 40


def _to_token_tiles(ref, val, ts):
    for c in range(val.shape[1] // LANES):
        ref[pl.ds(c, ts, stride=SUBLANES), :] = val[:, c * LANES:(c + 1) * LANES]


def _from_token_tiles(ref, ts, d):
    return jnp.concatenate(
        [ref[pl.ds(c, ts, stride=SUBLANES), :] for c in range(d // LANES)], axis=1)


def _post_kernel(x_ref, yc_ref, ysb_ref, kb_ref, vb_ref, wo_ref, wq_ref, wmo_ref,
                 ln_ref, wr_ref, br_ref, tri_ref,
                 x2_ref, route_ref, cnt_ref, carry_ref, *, ts):
    first = jnp.logical_and(pl.program_id(0) == 0, pl.program_id(1) == 0)
    d = x_ref.shape[2]
    cw = yc_ref.shape[2]
    x = x_ref[0]
    h1 = _dot(yc_ref[0], wo_ref[0:cw, :]) + _dot(ysb_ref[0], wo_ref[cw:, :])
    x1 = _layer_norm(ALPHA * x + h1, ln_ref[0:1, :], ln_ref[1:2, :])

    hd = d // MEM_HEADS
    q = (_dot(x1.astype(BF16), wq_ref[...]) * (hd ** -0.5)).astype(BF16)
    outs = []
    for h in range(MEM_HEADS):
        sc = _dot_nt(q[:, h * hd:(h + 1) * hd], kb_ref[0, :, h * hd:(h + 1) * hd])
        p = jnp.exp(sc - jnp.max(sc, axis=-1, keepdims=True))
        den = jnp.sum(p, axis=-1, keepdims=True)
        o = _dot(p.astype(BF16), vb_ref[0, :, h * hd:(h + 1) * hd])
        outs.append((o * (1.0 / den)).astype(BF16))
    o = jnp.concatenate(outs, axis=1)
    h2 = _dot(o, wmo_ref[...])
    x2 = _layer_norm(ALPHA * x1 + h2, ln_ref[2:3, :], ln_ref[3:4, :])
    _to_token_tiles(x2_ref, x2, ts)

    lt = _dot_nt(wr_ref[...], x2.astype(BF16)) + br_ref[:, 0:1]
    gl = [lt[g:g + 1, :] for g in range(N_GROUPS)]
    gmax = functools.reduce(jnp.maximum, gl)
    gden = functools.reduce(lambda a, b: a + b, [jnp.exp(l - gmax) for l in gl])
    gprob = 1.0 / gden
    grp = jnp.full(gmax.shape, N_GROUPS - 1, jnp.int32)
    for g in reversed(range(N_GROUPS - 1)):
        grp = jnp.where(gl[g] >= gmax, g, grp)
    el = lt[N_GROUPS:N_GROUPS + EXPERTS_PER_GROUP, :]
    for g in range(1, N_GROUPS):
        lo = N_GROUPS + g * EXPERTS_PER_GROUP
        el = jnp.where(grp == g, lt[lo:lo + EXPERTS_PER_GROUP, :], el)
    row = lax.broadcasted_iota(jnp.int32, el.shape, 0)
    t1 = jnp.max(el, axis=0, keepdims=True)
    i1 = jnp.min(jnp.where(el >= t1, row, EXPERTS_PER_GROUP), axis=0, keepdims=True)
    el2 = jnp.where(row == i1, -jnp.inf, el)
    t2 = jnp.max(el2, axis=0, keepdims=True)
    i2 = jnp.min(jnp.where(el2 >= t2, row, EXPERTS_PER_GROUP), axis=0, keepdims=True)
    e2w = jnp.exp(t2 - t1)
    g0 = gprob / (1.0 + e2w)
    g1 = gprob * e2w / (1.0 + e2w)
    e0 = grp * EXPERTS_PER_GROUP + i1
    e1 = grp * EXPERTS_PER_GROUP + i2

    @pl.when(first)
    def _():
        carry_ref[...] = jnp.zeros(carry_ref.shape, F32)

    erow = lax.broadcasted_iota(jnp.int32, (N_EXPERTS, ts), 0)
    oh0 = erow == e0
    oh1 = erow == e1
    both = jnp.logical_or(oh0, oh1)
    tot = _dot(jnp.where(both, 1.0, 0.0).astype(BF16), tri_ref[...])
    before = tot[:, :ts] + carry_ref[:, 0:1]
    r0 = jnp.sum(jnp.where(oh0, before, 0.0), axis=0, keepdims=True)
    r1 = jnp.sum(jnp.where(oh1, before, 0.0), axis=0, keepdims=True)
    carry_ref[...] = carry_ref[...] + tot[:, ts:]
    cnt_ref[...] = carry_ref[...]

    zero = jnp.zeros_like(g0)
    route_ref[0] = jnp.concatenate(
        [e0.astype(F32), e1.astype(F32), r0, r1, g0, g1, zero, zero], axis=0)


def _post_mixer(x, yc, ysb, kb, vb, wo_b, wq_b, wmo_b, ln, wr_t, br, ts):
    bn, sl, d = x.shape
    cw = yc.shape[2]
    ml = kb.shape[1]
    nt = sl // ts
    j = np.arange(ts)
    tri = jnp.asarray(np.concatenate([j[:, None] < j[None, :], np.ones((ts, LANES), bool)], axis=1), BF16)
    tile = lambda w: pl.BlockSpec((1, ts, w), lambda b, s: (b, s, 0))
    full = lambda a: pl.BlockSpec(a.shape, lambda b, s: (0,) * a.ndim)
    kvspec = pl.BlockSpec((1, ml, d), lambda b, s: (b, 0, 0))
    return pl.pallas_call(
        functools.partial(_post_kernel, ts=ts),
        grid=(bn, nt),
        in_specs=[tile(d), tile(cw), tile(ysb.shape[2]), kvspec, kvspec,
                  full(wo_b), full(wq_b), full(wmo_b), full(ln), full(wr_t), full(br), full(tri)],
        out_specs=[
            pl.BlockSpec((ts * SUBLANES, LANES), lambda b, s: (b * nt + s, 0)),
            pl.BlockSpec((1, ROUTE_ROWS, ts), lambda b, s: (b * nt + s, 0, 0)),
            pl.BlockSpec((N_EXPERTS, LANES), lambda b, s: (0, 0)),
        ],
        out_shape=[
            jax.ShapeDtypeStruct((bn * sl * SUBLANES, LANES), F32),
            jax.ShapeDtypeStruct((bn * nt, ROUTE_ROWS, ts), F32),
            jax.ShapeDtypeStruct((N_EXPERTS, LANES), F32),
        ],
        scratch_shapes=[pltpu.VMEM((N_EXPERTS, LANES), F32)],
        compiler_params=pltpu.CompilerParams(
            dimension_semantics=("arbitrary", "arbitrary"), vmem_limit_bytes=VMEM_LIMIT),
        name="post_mixer",
    )(x, yc, ysb, kb, vb, wo_b, wq_b, wmo_b, ln, wr_t, br, tri)


def _router_params(ln1_g, ln1_b, ln2_g, ln2_b, w_rg, b_rg, w_re, b_re):
    d = w_rg.shape[0]
    ln = jnp.stack([ln1_g, ln1_b, ln2_g, ln2_b]).astype(F32)
    pad = ROUTER_ROWS - N_GROUPS - N_EXPERTS
    wr_t = jnp.concatenate([w_rg.T, w_re.T, jnp.zeros((pad, d), w_rg.dtype)], axis=0).astype(BF16)
    br = jnp.concatenate([b_rg, b_re, jnp.zeros((pad,), b_rg.dtype)]).astype(F32)
    br = jnp.broadcast_to(br[:, None], (ROUTER_ROWS, LANES))
    return ln, wr_t, br


def _row_copy(src, src_row, dst, dst_row, sem):
    return pltpu.make_async_copy(src.at[pl.ds(src_row * SUBLANES, SUBLANES), :],
                                 dst.at[pl.ds(dst_row * SUBLANES, SUBLANES), :], sem)


def _dispatch_kernel(dest_ref, x2_ref, xs_in_ref, xs_ref, sem, *, td):
    del xs_in_ref

    def issue(i, c):
        for k in range(TOP_K):
            _row_copy(x2_ref, i, xs_ref, dest_ref[0, 0, k * td + i], sem).start()
        return c

    lax.fori_loop(0, td, issue, 0)

    def drain(i, c):
        _row_copy(x2_ref, 0, xs_ref, 0, sem).wait()
        return c

    lax.fori_loop(0, TOP_K * td, drain, 0)


def _dispatch(dest, x2t, xs_init, td):
    nt = dest.shape[0]
    return pl.pallas_call(
        functools.partial(_dispatch_kernel, td=td),
        grid=(nt,),
        in_specs=[
            pl.BlockSpec((1, 1, TOP_K * td), lambda i: (i, 0, 0), memory_space=pltpu.SMEM),
            pl.BlockSpec((td * SUBLANES, LANES), lambda i: (i, 0)),
            pl.BlockSpec(memory_space=pl.ANY),
        ],
        out_specs=pl.BlockSpec(memory_space=pl.ANY),
        out_shape=jax.ShapeDtypeStruct(xs_init.shape, xs_init.dtype),
        scratch_shapes=[pltpu.SemaphoreType.DMA(())],
        input_output_aliases={2: 0},
        compiler_params=pltpu.CompilerParams(
            dimension_semantics=("arbitrary",), vmem_limit_bytes=VMEM_LIMIT,
            has_side_effects=True),
        name="dispatch",
    )(dest, x2t, xs_init)


def _expert_kernel(be_ref, nu_ref, xs_ref, wg_ref, wu_ref, wd_ref, ys_ref, *, tm):
    del be_ref
    d = wg_ref.shape[1]

    @pl.when(pl.program_id(0) < nu_ref[0])
    def _():
        xb = _from_token_tiles(xs_ref, tm, d).astype(BF16)
        g = _dot(xb, wg_ref[0])
        u = _dot(xb, wu_ref[0])
        h = g * jax.nn.sigmoid(g) * u
        _to_token_tiles(ys_ref, _dot(h.astype(BF16), wd_ref[0]), tm)

    @pl.when(pl.program_id(0) >= nu_ref[0])
    def _():
        ys_ref[...] = jnp.zeros(ys_ref.shape, ys_ref.dtype)


def _experts(block_expert, n_used, xs, wg_b, wu_b, wd_b, tm):
    ne, d, f = wg_b.shape
    nb = xs.shape[0] // (tm * SUBLANES)
    rows = lambda i, be, nu: (i, 0)
    wsel = lambda i, be, nu: (be[i], 0, 0)
    return pl.pallas_call(
        functools.partial(_expert_kernel, tm=tm),
        grid_spec=pltpu.PrefetchScalarGridSpec(
            num_scalar_prefetch=2,
            grid=(nb,),
            in_specs=[
                pl.BlockSpec((tm * SUBLANES, LANES), rows),
                pl.BlockSpec((1, d, f), wsel),
                pl.BlockSpec((1, d, f), wsel),
                pl.BlockSpec((1, f, d), wsel),
            ],
            out_specs=pl.BlockSpec((tm * SUBLANES, LANES), rows),
        ),
        out_shape=jax.ShapeDtypeStruct(xs.shape, F32),
        compiler_params=pltpu.CompilerParams(
            dimension_semantics=("arbitrary",), vmem_limit_bytes=VMEM_LIMIT),
        name="experts",
    )(block_expert, n_used, xs, wg_b, wu_b, wd_b)


def _combine_kernel(dest_ref, x2_ref, gate_ref, ln_ref, ys_ref, o_ref, ybuf_ref, sem, *, tc):
    d = o_ref.shape[1]

    def issue(i, c):
        for k in range(TOP_K):
            _row_copy(ys_ref, dest_ref[0, 0, k * tc + i], ybuf_ref.at[k], i, sem).start()
        return c

    lax.fori_loop(0, tc, issue, 0)

    def drain(i, c):
        _row_copy(ys_ref, 0, ybuf_ref.at[0], 0, sem).wait()
        return c

    lax.fori_loop(0, TOP_K * tc, drain, 0)

    moe = (gate_ref[:, 0:1] * _from_token_tiles(ybuf_ref.at[0], tc, d)
           + gate_ref[:, 1:2] * _from_token_tiles(ybuf_ref.at[1], tc, d))
    r = ALPHA * _from_token_tiles(x2_ref, tc, d) + moe
    o_ref[...] = _layer_norm(r, ln_ref[0:1, :], ln_ref[1:2, :])


def _combine(dest, x2t, gates, ln3, ys, tc):
    nt = dest.shape[0]
    n = gates.shape[0]
    d = ln3.shape[1]
    return pl.pallas_call(
        functools.partial(_combine_kernel, tc=tc),
        grid=(nt,),
        in_specs=[
            pl.BlockSpec((1, 1, TOP_K * tc), lambda i: (i, 0, 0), memory_space=pltpu.SMEM),
            pl.BlockSpec((tc * SUBLANES, LANES), lambda i: (i, 0)),
            pl.BlockSpec((tc, TOP_K), lambda i: (i, 0)),
            pl.BlockSpec(ln3.shape, lambda i: (0, 0)),
            pl.BlockSpec(memory_space=pl.ANY),
        ],
        out_specs=pl.BlockSpec((tc, d), lambda i: (i, 0)),
        out_shape=jax.ShapeDtypeStruct((n, d), F32),
        scratch_shapes=[pltpu.VMEM((TOP_K, tc * SUBLANES, LANES), F32),
                        pltpu.SemaphoreType.DMA(())],
        compiler_params=pltpu.CompilerParams(
            dimension_semantics=("arbitrary",), vmem_limit_bytes=VMEM_LIMIT),
        name="combine",
    )(dest, x2t, gates, ln3, ys)


def _routing_plan(route, counts, tm, tile):
    nt, _, ts = route.shape
    n = nt * ts
    cnt = counts[:, 0].astype(jnp.int32)
    padded = (cnt + tm - 1) // tm * tm
    pend = jnp.cumsum(padded)
    pstart = pend - padded
    rows = lambda r: route[:, r, :].reshape(n)
    dests = [jnp.take(pstart, rows(k).astype(jnp.int32)) + rows(2 + k).astype(jnp.int32)
             for k in range(TOP_K)]
    dest = jnp.stack([dk.reshape(n // tile, tile) for dk in dests], axis=1)
    dest = dest.reshape(n // tile, 1, TOP_K * tile)
    gates = jnp.stack([rows(4), rows(5)], axis=1)
    nb = n * TOP_K // tm + N_EXPERTS
    block_expert = jnp.minimum(
        jnp.searchsorted(pend, jnp.arange(nb, dtype=jnp.int32) * tm, side='right'),
        N_EXPERTS - 1).astype(jnp.int32)
    n_used = (pend[-1:] // tm).astype(jnp.int32)
    return dest, gates, block_expert, n_used, nb


def _moe(x2t, route, counts, wg_b, wu_b, wd_b, ln3, tm, tile):
    dest, gates, block_expert, n_used, nb = _routing_plan(route, counts, tm, tile)
    xs0 = jnp.zeros((nb * tm * SUBLANES, LANES), F32)
    xs = _dispatch(dest, x2t, xs0, tile)
    ys = _experts(block_expert, n_used, xs, wg_b, wu_b, wd_b, tm)
    return _combine(dest, x2t, gates, ln3, ys, tile)


def kernel(x, mem, w_in, conv_w, gn_conv, gn_sb, w_out, ln1_g, ln1_b, w_mq, w_mk, w_mv, w_mo,
           ln2_g, ln2_b, w_router_group, b_router_group, w_router_expert, b_router_expert,
           w_gate, w_up, w_down, ln3_g, ln3_b):
    bn, sl, d = x.shape
    for l in range(DEPTH):
        x = _layer(x, mem, w_in[l], conv_w[l], gn_conv[l], gn_sb[l], w_out[l], ln1_g[l], ln1_b[l],
                   w_mq[l], w_mk[l], w_mv[l], w_mo[l], ln2_g[l], ln2_b[l],
                   w_router_group[l], b_router_group[l], w_router_expert[l], b_router_expert[l],
                   w_gate[l], w_up[l], w_down[l], ln3_g[l], ln3_b[l])
    return x


def _layer(x, mem, w_in, conv_w, gn_conv, gn_sb, w_out, ln1_g, ln1_b, w_mq, w_mk, w_mv, w_mo,
           ln2_g, ln2_b, w_rg, b_rg, w_re, b_re, w_gate, w_up, w_down, ln3_g, ln3_b,
           ts=512, tq=256, tm=512, kv_nb=4):
    bn, sl, d = x.shape
    bf = lambda w: w.astype(BF16)
    yc, q, k, v = _proj_conv(x, bf(w_in), conv_w, gn_conv, ts)
    ysb = _sb_attention(q, k, v, gn_sb, tq)
    kb, vb = _kv_proj(mem, bf(w_mk), bf(w_mv), kv_nb)
    ln, wr_t, br = _router_params(ln1_g, ln1_b, ln2_g, ln2_b, w_rg, b_rg, w_re, b_re)
    x2t, route, counts = _post_mixer(x, yc, ysb, kb, vb, bf(w_out), bf(w_mq), bf(w_mo),
                                     ln, wr_t, br, ts)
    ln3 = jnp.stack([ln3_g, ln3_b]).astype(F32)
    out = _moe(x2t, route, counts, bf(w_gate), bf(w_up), bf(w_down), ln3, tm, ts)
    return out.reshape(bn, sl, d)
```

```python
import functools

import jax
import jax.numpy as jnp
import numpy as np
from jax import lax
from jax.experimental import pallas as pl
from jax.experimental.pallas import tpu as pltpu

CONV_WIDTH = 512
CONV_GROUP = 64
CONV_K = 3
SB_HEADS = 8
SB_HEAD_DIM = 64
SB_WIDTH = SB_HEADS * SB_HEAD_DIM
MEM_HEADS = 4
N_GROUPS = 4
EXPERTS_PER_GROUP = 8
N_EXPERTS = N_GROUPS * EXPERTS_PER_GROUP
TOP_K = 2
DEPTH = 1
ALPHA = (2 * DEPTH) ** 0.25
LN_EPS = 1e-5
RMS_EPS = 1e-6

LANES = 128
SUBLANES = 8
VMEM_LIMIT = 56 * 1024 * 1024

BF16 = jnp.bfloat16
F32 = jnp.float32


def _split_bf16(a):
    hi = a.astype(BF16)
    lo = (a - hi.astype(F32)).astype(BF16)
    return hi, lo


def _dot(a, b):
    return jnp.dot(a, b, preferred_element_type=F32)


def _dot_nt(a, b):
    return lax.dot_general(a, b, (((1,), (1,)), ((), ())), preferred_element_type=F32)


def _layer_norm(r, g, b):
    mu = jnp.mean(r, axis=-1, keepdims=True)
    c = r - mu
    var = jnp.mean(c * c, axis=-1, keepdims=True)
    return c * lax.rsqrt(var + LN_EPS) * g + b


def _proj_conv_kernel(x_ref, w_ref, cw_ref, gn_ref, gmat_ref,
                      yc_ref, q_ref, k_ref, v_ref, ubuf_ref, *, ts):
    s = pl.program_id(1)
    xb = x_ref[0].astype(BF16)
    cw = CONV_WIDTH

    @pl.when(s == 0)
    def _():
        ubuf_ref[0:SUBLANES, :] = jnp.zeros((SUBLANES, cw), F32)

    cc = _dot(xb, w_ref[:, cw:2 * cw])
    ch = _dot(xb, w_ref[:, 2 * cw:3 * cw])
    u = cc * ch
    ubuf_ref[SUBLANES:SUBLANES + ts, :] = u
    um1 = ubuf_ref[SUBLANES - 1:SUBLANES - 1 + ts, :]
    um2 = ubuf_ref[SUBLANES - 2:SUBLANES - 2 + ts, :]
    conv = cw_ref[2:3, :] * u + cw_ref[1:2, :] * um1 + cw_ref[0:1, :] * um2
    ubuf_ref[0:SUBLANES, :] = ubuf_ref[ts:ts + SUBLANES, :]

    cb = _dot(xb, w_ref[:, 0:cw])
    y = cb * conv
    hi, lo = _split_bf16(y * y)
    gsum = _dot(hi, gmat_ref[...]) + _dot(lo, gmat_ref[...])
    yn = y * lax.rsqrt(gsum * (1.0 / CONV_GROUP) + RMS_EPS) * gn_ref[...]
    yc_ref[0] = yn.astype(yc_ref.dtype)

    sw = SB_WIDTH
    base = 3 * cw
    q_ref[0] = (_dot(xb, w_ref[:, base:base + sw]) * (SB_HEAD_DIM ** -0.5)).astype(q_ref.dtype)
    k_ref[0] = _dot(xb, w_ref[:, base + sw:base + 2 * sw]).astype(k_ref.dtype)
    v_ref[0] = _dot(xb, w_ref[:, base + 2 * sw:base + 3 * sw]).astype(v_ref.dtype)


def _proj_conv(x, w_in_b, conv_w, gn_conv, ts):
    bn, sl, d = x.shape
    gidx = np.arange(CONV_WIDTH) // CONV_GROUP
    gmat = jnp.asarray(gidx[:, None] == gidx[None, :], BF16)
    outs = [jax.ShapeDtypeStruct((bn, sl, CONV_WIDTH), BF16)] * 4
    tile = lambda w: pl.BlockSpec((1, ts, w), lambda b, s: (b, s, 0))
    full = lambda a: pl.BlockSpec(a.shape, lambda b, s: (0,) * a.ndim)
    gn = gn_conv.reshape(1, CONV_WIDTH)
    return pl.pallas_call(
        functools.partial(_proj_conv_kernel, ts=ts),
        grid=(bn, sl // ts),
        in_specs=[tile(d), full(w_in_b), full(conv_w), full(gn), full(gmat)],
        out_specs=[tile(CONV_WIDTH)] * 4,
        out_shape=outs,
        scratch_shapes=[pltpu.VMEM((ts + 2 * SUBLANES, CONV_WIDTH), F32)],
        compiler_params=pltpu.CompilerParams(
            dimension_semantics=("arbitrary", "arbitrary"), vmem_limit_bytes=VMEM_LIMIT),
        name="proj_conv",
    )(x, w_in_b, conv_w, gn, gmat)


SB_TK = 128
SB_UNROLL = 4
LOG2E = 1.4426950408889634
BF16_BITS_OF_F32 = 0xFFFF0000


def _bits(a):
    return lax.bitcast_convert_type(a, jnp.uint32)


def _softplus2(z2):
    pos = jnp.maximum(z2, 0.0)
    return pos + jnp.log2(1.0 + jnp.exp2(z2 - 2.0 * pos))


def _sb_kernel(q_ref, k_ref, v_ref, gn_ref, cum_ref, o_ref, acc_ref, run_ref, *, tq):
    qi = pl.program_id(2)
    tk = SB_TK
    nd = tq // tk
    lane = lax.broadcasted_iota(jnp.int32, (1, LANES), 1)
    head0 = lane < SB_HEAD_DIM
    zero = jnp.zeros((), BF16)

    def per_head_rows(ref, kstart):
        blk = ref[0, pl.ds(kstart, tk), :]
        return jnp.concatenate([jnp.where(head0, blk, zero), jnp.where(head0, zero, blk)], axis=0)

    def tile(row0, kstart, diag, runs):
        rows = tq - row0
        z2 = _dot_nt(q_ref[0, row0:, :], per_head_rows(k_ref, kstart)) * LOG2E
        if diag:
            t_loc = lax.broadcasted_iota(jnp.int32, (tk, tk), 0)
            s_loc = lax.broadcasted_iota(jnp.int32, (tk, tk), 1)
            before = s_loc < t_loc
            causal = lambda m: jnp.concatenate(
                [jnp.where(before, m[:tk], 0.0)] + ([m[tk:]] if rows > tk else []), axis=0)
        a_parts, new_runs = [], []
        for h in range(2):
            zh = z2[:, h * tk:(h + 1) * tk]
            sp = _softplus2(zh)
            if diag:
                sp = causal(sp)
            hi = lax.bitcast_convert_type(_bits(sp) & jnp.uint32(BF16_BITS_OF_F32), F32)
            lhs = jnp.concatenate([hi.astype(BF16), (sp - hi).astype(BF16)], axis=1)
            cs = _dot(lhs, cum_ref[...])
            a = jnp.exp2(zh - cs[:, :tk] - runs[h])
            if diag:
                a = causal(a)
            a_parts.append(a.astype(BF16))
            new_runs.append(runs[h] + cs[:, tk:])
        return jnp.concatenate(a_parts, axis=1), new_runs

    acc = None
    runs = None
    for d in reversed(range(nd)):
        row0 = d * tk
        kstart = pl.multiple_of(qi * tq + row0, tk)
        fresh = jnp.zeros((tk, LANES), F32)
        runs = [fresh, fresh] if runs is None else [jnp.concatenate([fresh, r], axis=0) for r in runs]
        a, runs = tile(row0, kstart, True, runs)
        av = _dot(a, per_head_rows(v_ref, kstart))
        acc = av if acc is None else av + jnp.concatenate([fresh, acc], axis=0)
    acc_ref[...] = acc
    run_ref[0] = runs[0]
    run_ref[1] = runs[1]

    def body(j, carry):
        runs = [run_ref[0], run_ref[1]]
        a_parts, v_parts = [], []
        for u in range(SB_UNROLL):
            kstart = pl.multiple_of((qi * nd - 1 - (j * SB_UNROLL + u)) * tk, tk)
            a, runs = tile(0, kstart, False, runs)
            a_parts.append(a)
            v_parts.append(per_head_rows(v_ref, kstart))
        acc_ref[...] += _dot(jnp.concatenate(a_parts, axis=1), jnp.concatenate(v_parts, axis=0))
        run_ref[0] = runs[0]
        run_ref[1] = runs[1]
        return carry

    lax.fori_loop(0, qi * nd // SB_UNROLL, body, 0)

    y = acc_ref[...]
    y2 = y * y
    s0 = jnp.sum(jnp.where(head0, y2, 0.0), axis=-1, keepdims=True)
    s1 = jnp.sum(jnp.where(head0, 0.0, y2), axis=-1, keepdims=True)
    ms = jnp.where(head0, s0, s1) * (1.0 / SB_HEAD_DIM)
    o_ref[0] = (y * lax.rsqrt(ms + RMS_EPS) * gn_ref[...]).astype(o_ref.dtype)


def _sb_attention(q, k, v, gn_sb, tq):
    bn, sl, w = q.shape
    npair = w // LANES
    j = np.arange(SB_TK)
    assert (tq // SB_TK) % SB_UNROLL == 0
    cum = np.concatenate([(j[:, None] >= j[None, :]), np.ones((SB_TK, SB_TK), bool)], axis=1)
    cum = jnp.asarray(np.concatenate([cum, cum], axis=0), BF16)
    gn = gn_sb.reshape(1, w)
    return pl.pallas_call(
        functools.partial(_sb_kernel, tq=tq),
        grid=(bn, npair, sl // tq),
        in_specs=[
            pl.BlockSpec((1, tq, LANES), lambda b, p, i: (b, i, p)),
            pl.BlockSpec((1, sl, LANES), lambda b, p, i: (b, 0, p)),
            pl.BlockSpec((1, sl, LANES), lambda b, p, i: (b, 0, p)),
            pl.BlockSpec((1, LANES), lambda b, p, i: (0, p)),
            pl.BlockSpec(cum.shape, lambda b, p, i: (0, 0)),
        ],
        out_specs=pl.BlockSpec((1, tq, LANES), lambda b, p, i: (b, i, p)),
        out_shape=jax.ShapeDtypeStruct((bn, sl, w), BF16),
        scratch_shapes=[pltpu.VMEM((tq, LANES), F32), pltpu.VMEM((2, tq, LANES), F32)],
        compiler_params=pltpu.CompilerParams(
            dimension_semantics=("arbitrary", "arbitrary", "arbitrary"),
            vmem_limit_bytes=VMEM_LIMIT),
        name="sb_attention",
    )(q, k, v, gn, cum)


def _kv_kernel(mem_ref, wk_ref, wv_ref, k_ref, v_ref):
    nb, ml, d = mem_ref.shape
    m = mem_ref[...].reshape(nb * ml, d).astype(BF16)
    k_ref[...] = _dot(m, wk_ref[...]).reshape(nb, ml, d).astype(k_ref.dtype)
    v_ref[...] = _dot(m, wv_ref[...]).reshape(nb, ml, d).astype(v_ref.dtype)


def _kv_proj(mem, wk_b, wv_b, nb):
    bn, ml, d = mem.shape
    blk = pl.BlockSpec((nb, ml, d), lambda i: (i, 0, 0))
    full = pl.BlockSpec((d, d), lambda i: (0, 0))
    return pl.pallas_call(
        _kv_kernel,
        grid=(bn // nb,),
        in_specs=[blk, full, full],
        out_specs=[blk, blk],
        out_shape=[jax.ShapeDtypeStruct((bn, ml, d), BF16)] * 2,
        compiler_params=pltpu.CompilerParams(
            dimension_semantics=("arbitrary",), vmem_limit_bytes=VMEM_LIMIT),
        name="kv_proj",
    )(mem, wk_b, wv_b)


ROUTE_ROWS = 8
ROUTER_ROWS = 40


def _to_token_tiles(ref, val, ts):
    for c in range(val.shape[1] // LANES):
        ref[pl.ds(c, ts, stride=SUBLANES), :] = val[:, c * LANES:(c + 1) * LANES]


def _from_token_tiles(ref, ts, d):
    return jnp.concatenate(
        [ref[pl.ds(c, ts, stride=SUBLANES), :] for c in range(d // LANES)], axis=1)


def _post_kernel(x_ref, yc_ref, ysb_ref, kb_ref, vb_ref, wo_ref, wq_ref, wmo_ref,
                 ln_ref, wr_ref, br_ref, tri_ref, lt_ref,
                 x2_ref, route_ref, cnt_ref, *, ts):
    d = x_ref.shape[2]
    cw = yc_ref.shape[2]
    x = x_ref[0]
    h1 = _dot(yc_ref[0], wo_ref[0:cw, :]) + _dot(ysb_ref[0], wo_ref[cw:, :])
    x1 = _layer_norm(ALPHA * x + h1, ln_ref[0:1, :], ln_ref[1:2, :])

    hd = d // MEM_HEADS
    q = (_dot(x1.astype(BF16), wq_ref[...]) * (hd ** -0.5)).astype(BF16)
    outs = []
    for h in range(MEM_HEADS):
        sc = _dot_nt(q[:, h * hd:(h + 1) * hd], kb_ref[0, :, h * hd:(h + 1) * hd])
        p = jnp.exp(sc - jnp.max(sc, axis=-1, keepdims=True))
        den = jnp.sum(p, axis=-1, keepdims=True)
        o = _dot(p.astype(BF16), vb_ref[0, :, h * hd:(h + 1) * hd])
        outs.append((o * (1.0 / den)).astype(BF16))
    o = jnp.concatenate(outs, axis=1)
    h2 = _dot(o, wmo_ref[...])
    x2 = _layer_norm(ALPHA * x1 + h2, ln_ref[2:3, :], ln_ref[3:4, :])
    _to_token_tiles(x2_ref, x2, ts)

    lt = _dot_nt(wr_ref[...], x2.astype(BF16)) + br_ref[:, 0:1]
    gl = [lt[g:g + 1, :] for g in range(N_GROUPS)]
    gmax = functools.reduce(jnp.maximum, gl)
    gden = functools.reduce(lambda a, b: a + b, [jnp.exp(l - gmax) for l in gl])
    gprob = 1.0 / gden
    grp = jnp.full(gmax.shape, N_GROUPS - 1, jnp.int32)
    for g in reversed(range(N_GROUPS - 1)):
        grp = jnp.where(gl[g] >= gmax, g, grp)
    el = lt[N_GROUPS:N_GROUPS + EXPERTS_PER_GROUP, :]
    for g in range(1, N_GROUPS):
        lo = N_GROUPS + g * EXPERTS_PER_GROUP
        el = jnp.where(grp == g, lt[lo:lo + EXPERTS_PER_GROUP, :], el)
    row = lax.broadcasted_iota(jnp.int32, el.shape, 0)
    t1 = jnp.max(el, axis=0, keepdims=True)
    i1 = jnp.min(jnp.where(el >= t1, row, EXPERTS_PER_GROUP), axis=0, keepdims=True)
    el2 = jnp.where(row == i1, -jnp.inf, el)
    t2 = jnp.max(el2, axis=0, keepdims=True)
    i2 = jnp.min(jnp.where(el2 >= t2, row, EXPERTS_PER_GROUP), axis=0, keepdims=True)
    e2w = jnp.exp(t2 - t1)
    g0 = gprob / (1.0 + e2w)
    g1 = gprob * e2w / (1.0 + e2w)
    e0 = grp * EXPERTS_PER_GROUP + i1
    e1 = grp * EXPERTS_PER_GROUP + i2

    erow = lax.broadcasted_iota(jnp.int32, (N_EXPERTS, ts), 0)
    oh0 = erow == e0
    oh1 = erow == e1
    both = jnp.where(jnp.logical_or(oh0, oh1), 1.0, 0.0).astype(BF16)
    tot = _dot(both, tri_ref[...])
    lower = _dot(lt_ref[...], both).astype(BF16)
    offs = _dot(lower, tri_ref[:, ts:])
    pos = tot[:, :ts] + offs[:, 0:1]
    p0 = jnp.sum(jnp.where(oh0, pos, 0.0), axis=0, keepdims=True)
    p1 = jnp.sum(jnp.where(oh1, pos, 0.0), axis=0, keepdims=True)
    cnt_ref[0] = tot[:, ts:]

    zero = jnp.zeros_like(g0)
    route_ref[0] = jnp.concatenate(
        [e0.astype(F32), e1.astype(F32), p0, p1, g0, g1, zero, zero], axis=0)


def _post_mixer(x, yc, ysb, kb, vb, wo_b, wq_b, wmo_b, ln, wr_t, br, ts):
    bn, sl, d = x.shape
    cw = yc.shape[2]
    ml = kb.shape[1]
    nt = sl // ts
    j = np.arange(ts)
    tri = jnp.asarray(np.concatenate([j[:, None] < j[None, :], np.ones((ts, LANES), bool)], axis=1), BF16)
    je = np.arange(N_EXPERTS)
    lt = jnp.asarray(je[None, :] < je[:, None], BF16)
    tile = lambda w: pl.BlockSpec((1, ts, w), lambda b, s: (b, s, 0))
    full = lambda a: pl.BlockSpec(a.shape, lambda b, s: (0,) * a.ndim)
    kvspec = pl.BlockSpec((1, ml, d), lambda b, s: (b, 0, 0))
    return pl.pallas_call(
        functools.partial(_post_kernel, ts=ts),
        grid=(bn, nt),
        in_specs=[tile(d), tile(cw), tile(ysb.shape[2]), kvspec, kvspec,
                  full(wo_b), full(wq_b), full(wmo_b), full(ln), full(wr_t), full(br), full(tri),
                  full(lt)],
        out_specs=[
            pl.BlockSpec((ts * SUBLANES, LANES), lambda b, s: (b * nt + s, 0)),
            pl.BlockSpec((1, ROUTE_ROWS, ts), lambda b, s: (b * nt + s, 0, 0)),
            pl.BlockSpec((1, N_EXPERTS, LANES), lambda b, s: (b * nt + s, 0, 0)),
        ],
        out_shape=[
            jax.ShapeDtypeStruct((bn * sl * SUBLANES, LANES), F32),
            jax.ShapeDtypeStruct((bn * nt, ROUTE_ROWS, ts), F32),
            jax.ShapeDtypeStruct((bn * nt, N_EXPERTS, LANES), F32),
        ],
        compiler_params=pltpu.CompilerParams(
            dimension_semantics=("arbitrary", "arbitrary"), vmem_limit_bytes=VMEM_LIMIT),
        name="post_mixer",
    )(x, yc, ysb, kb, vb, wo_b, wq_b, wmo_b, ln, wr_t, br, tri, lt)


def _router_params(ln1_g, ln1_b, ln2_g, ln2_b, w_rg, b_rg, w_re, b_re):
    d = w_rg.shape[0]
    ln = jnp.stack([ln1_g, ln1_b, ln2_g, ln2_b]).astype(F32)
    pad = ROUTER_ROWS - N_GROUPS - N_EXPERTS
    wr_t = jnp.concatenate([w_rg.T, w_re.T, jnp.zeros((pad, d), w_rg.dtype)], axis=0).astype(BF16)
    br = jnp.concatenate([b_rg, b_re, jnp.zeros((pad,), b_rg.dtype)]).astype(F32)
    br = jnp.broadcast_to(br[:, None], (ROUTER_ROWS, LANES))
    return ln, wr_t, br


def _chunk_copies(plan_ref, w, hbm_ref, buf_ref, sems, to_hbm, nbits, act):
    def per_expert(e, c):
        off = plan_ref[w, e]
        cnt = plan_ref[w, N_EXPERTS + e]
        dst = plan_ref[w, 2 * N_EXPERTS + e]
        for b in range(nbits):
            @pl.when(((cnt >> b) & 1) == 1)
            def _(b=b):
                done = (cnt >> (b + 1)) << (b + 1)
                n = (1 << b) * SUBLANES
                loc = buf_ref.at[pl.ds(pl.multiple_of((off + done) * SUBLANES, SUBLANES), n), :]
                glob = hbm_ref.at[pl.ds(pl.multiple_of((dst + done) * SUBLANES, SUBLANES), n), :]
                cp = (pltpu.make_async_copy(loc, glob, sems.at[b]) if to_hbm
                      else pltpu.make_async_copy(glob, loc, sems.at[b]))
                act(cp)
        return c

    lax.fori_loop(0, N_EXPERTS, per_expert, 0)


TOKEN_UNROLL = 8


def _dispatch_kernel(plan_ref, nu_ref, pos_ref, x2_ref, xs_ref, sbuf_ref, sems, *, td, tm, nbits):
    w = pl.program_id(0)
    last = pl.num_programs(0) - 1
    slot = w % 2
    scur_ref = sbuf_ref.at[slot]

    def send(win, act):
        s = win % 2
        _chunk_copies(plan_ref, win, xs_ref, sbuf_ref.at[s], sems.at[s], True, nbits, act)

    @pl.when(w == 0)
    def _():
        block = tm * SUBLANES
        zeros_ref = sbuf_ref.at[1]
        zeros_ref[0:block, :] = jnp.zeros((block, LANES), F32)
        pad_row = pl.num_programs(0)
        pads = lambda act: _chunk_copies(plan_ref, pad_row, xs_ref, zeros_ref, sems.at[1], True, nbits, act)
        pads(lambda cp: cp.start())
        pads(lambda cp: cp.wait())
        tail = lambda i: pltpu.make_async_copy(
            zeros_ref.at[0:block, :], xs_ref.at[pl.ds(pl.multiple_of(i * block, block), block), :],
            sems.at[1, 0])
        n_blocks = xs_ref.shape[0] // block
        lax.fori_loop(nu_ref[0], n_blocks, lambda i, c: (tail(i).start(), c)[1], 0)
        lax.fori_loop(nu_ref[0], n_blocks, lambda i, c: (tail(i).wait(), c)[1], 0)

    def place(i, c):
        for u in range(TOKEN_UNROLL):
            t = i * TOKEN_UNROLL + u
            tok = x2_ref[pl.ds(pl.multiple_of(t * SUBLANES, SUBLANES), SUBLANES), :]
            for k in range(TOP_K):
                p = pos_ref[0, 0, k * td + t]
                scur_ref[pl.ds(pl.multiple_of(p * SUBLANES, SUBLANES), SUBLANES), :] = tok
        return c

    lax.fori_loop(0, td // TOKEN_UNROLL, place, 0)
    send(w, lambda cp: cp.start())

    @pl.when(w > 0)
    def _():
        send(w - 1, lambda cp: cp.wait())

    @pl.when(w == last)
    def _():
        send(w, lambda cp: cp.wait())


def _dispatch(plan, n_used, pos, x2t, n_rows, td, tm):
    nt = pos.shape[0]
    nbits = td.bit_length()
    assert tm <= TOP_K * td and n_rows % tm == 0
    return pl.pallas_call(
        functools.partial(_dispatch_kernel, td=td, tm=tm, nbits=nbits),
        grid_spec=pltpu.PrefetchScalarGridSpec(
            num_scalar_prefetch=2,
            grid=(nt,),
            in_specs=[
                pl.BlockSpec((1, 1, TOP_K * td), lambda i, plan, nu: (i, 0, 0), memory_space=pltpu.SMEM),
                pl.BlockSpec((td * SUBLANES, LANES), lambda i, plan, nu: (i, 0)),
            ],
            out_specs=pl.BlockSpec(memory_space=pl.ANY),
            scratch_shapes=[pltpu.VMEM((2, TOP_K * td * SUBLANES, LANES), F32),
                            pltpu.SemaphoreType.DMA((2, nbits))],
        ),
        out_shape=jax.ShapeDtypeStruct((n_rows * SUBLANES, LANES), F32),
        compiler_params=pltpu.CompilerParams(
            dimension_semantics=("arbitrary",), vmem_limit_bytes=VMEM_LIMIT),
        name="dispatch",
    )(plan, n_used, pos, x2t)


def _expert_kernel(be_ref, nu_ref, xs_ref, wg_ref, wu_ref, wd_ref, ys_ref, *, tm):
    del be_ref
    d = wg_ref.shape[1]

    @pl.when(pl.program_id(0) < nu_ref[0])
    def _():
        xb = _from_token_tiles(xs_ref, tm, d).astype(BF16)
        g = _dot(xb, wg_ref[0])
        u = _dot(xb, wu_ref[0])
        h = g * jax.nn.sigmoid(g) * u
        _to_token_tiles(ys_ref, _dot(h.astype(BF16), wd_ref[0]), tm)

    @pl.when(pl.program_id(0) >= nu_ref[0])
    def _():
        ys_ref[...] = jnp.zeros(ys_ref.shape, ys_ref.dtype)


def _experts(block_expert, n_used, xs, wg_b, wu_b, wd_b, tm):
    ne, d, f = wg_b.shape
    nb = xs.shape[0] // (tm * SUBLANES)
    rows = lambda i, be, nu: (i, 0)
    wsel = lambda i, be, nu: (be[i], 0, 0)
    return pl.pallas_call(
        functools.partial(_expert_kernel, tm=tm),
        grid_spec=pltpu.PrefetchScalarGridSpec(
            num_scalar_prefetch=2,
            grid=(nb,),
            in_specs=[
                pl.BlockSpec((tm * SUBLANES, LANES), rows),
                pl.BlockSpec((1, d, f), wsel),
                pl.BlockSpec((1, d, f), wsel),
                pl.BlockSpec((1, f, d), wsel),
            ],
            out_specs=pl.BlockSpec((tm * SUBLANES, LANES), rows),
        ),
        out_shape=jax.ShapeDtypeStruct(xs.shape, F32),
        compiler_params=pltpu.CompilerParams(
            dimension_semantics=("arbitrary",), vmem_limit_bytes=VMEM_LIMIT),
        name="experts",
    )(block_expert, n_used, xs, wg_b, wu_b, wd_b)


def _combine_kernel(plan_ref, pos_ref, gate_ref, x2_ref, ln_ref, ys_ref, o_ref,
                    ybuf_ref, rbuf_ref, sems, *, tc, nbits):
    w = pl.program_id(0)
    d = o_ref.shape[1]
    slot = w % 2

    def fetch(win, act):
        s = win % 2
        _chunk_copies(plan_ref, win, ys_ref, ybuf_ref.at[s], sems.at[s], False, nbits, act)

    @pl.when(w == 0)
    def _():
        fetch(w, lambda cp: cp.start())

    @pl.when(w + 1 < pl.num_programs(0))
    def _():
        fetch(w + 1, lambda cp: cp.start())

    fetch(w, lambda cp: cp.wait())
    ycur_ref = ybuf_ref.at[slot]

    def gather(i, c):
        for u in range(TOKEN_UNROLL):
            t = i * TOKEN_UNROLL + u
            rows = pl.ds(pl.multiple_of(t * SUBLANES, SUBLANES), SUBLANES)
            moe = None
            for k in range(TOP_K):
                p = pos_ref[0, 0, k * tc + t]
                y = ycur_ref[pl.ds(pl.multiple_of(p * SUBLANES, SUBLANES), SUBLANES), :]
                term = gate_ref[0, 0, k * tc + t] * y
                moe = term if moe is None else moe + term
            rbuf_ref[rows, :] = ALPHA * x2_ref[rows, :] + moe
        return c

    lax.fori_loop(0, tc // TOKEN_UNROLL, gather, 0)
    o_ref[...] = _layer_norm(_from_token_tiles(rbuf_ref, tc, d), ln_ref[0:1, :], ln_ref[1:2, :])


def _combine(plan, pos, gates, x2t, ln3, ys, tc):
    nt = pos.shape[0]
    d = ln3.shape[1]
    nbits = tc.bit_length()
    smem_tile = pl.BlockSpec((1, 1, TOP_K * tc), lambda i, plan: (i, 0, 0), memory_space=pltpu.SMEM)
    return pl.pallas_call(
        functools.partial(_combine_kernel, tc=tc, nbits=nbits),
        grid_spec=pltpu.PrefetchScalarGridSpec(
            num_scalar_prefetch=1,
            grid=(nt,),
            in_specs=[
                smem_tile,
                smem_tile,
                pl.BlockSpec((tc * SUBLANES, LANES), lambda i, plan: (i, 0)),
                pl.BlockSpec(ln3.shape, lambda i, plan: (0, 0)),
                pl.BlockSpec(memory_space=pl.ANY),
            ],
            out_specs=pl.BlockSpec((tc, d), lambda i, plan: (i, 0)),
            scratch_shapes=[pltpu.VMEM((2, TOP_K * tc * SUBLANES, LANES), F32),
                            pltpu.VMEM((tc * SUBLANES, LANES), F32),
                            pltpu.SemaphoreType.DMA((2, nbits))],
        ),
        out_shape=jax.ShapeDtypeStruct((nt * tc, d), F32),
        compiler_params=pltpu.CompilerParams(
            dimension_semantics=("arbitrary",), vmem_limit_bytes=VMEM_LIMIT),
        name="combine",
    )(plan, pos, gates, x2t, ln3, ys)


def _routing_plan(route, wcnt, tm):
    nt, _, ts = route.shape
    cw = wcnt[:, :, 0].astype(jnp.int32)
    cnt = jnp.sum(cw, axis=0)
    padded = (cnt + tm - 1) // tm * tm
    pend = jnp.cumsum(padded)
    pstart = pend - padded
    base = jnp.cumsum(cw, axis=0) - cw
    offs = jnp.cumsum(cw, axis=1) - cw
    plan = jnp.concatenate([offs, cw, pstart[None, :] + base], axis=1)
    pad_plan = jnp.concatenate([jnp.zeros_like(cnt), padded - cnt, pstart + cnt])
    plan = jnp.concatenate([plan, pad_plan[None, :]], axis=0)
    pos = route[:, 2:4, :].astype(jnp.int32).reshape(nt, 1, TOP_K * ts)
    gates = route[:, 4:6, :].reshape(nt, 1, TOP_K * ts)
    nb = nt * ts * TOP_K // tm + N_EXPERTS
    first_row = jnp.arange(nb, dtype=jnp.int32) * tm
    block_expert = jnp.minimum(
        jnp.sum((pend[None, :] <= first_row[:, None]).astype(jnp.int32), axis=1), N_EXPERTS - 1)
    n_used = (pend[-1:] // tm).astype(jnp.int32)
    return plan, pos, gates, block_expert, n_used, nb


def _moe(x2t, route, wcnt, wg_b, wu_b, wd_b, ln3, tm):
    ts = route.shape[2]
    plan, pos, gates, block_expert, n_used, nb = _routing_plan(route, wcnt, tm)
    xs = _dispatch(plan, n_used, pos, x2t, nb * tm, ts, tm)
    ys = _experts(block_expert, n_used, xs, wg_b, wu_b, wd_b, tm)
    return _combine(plan, pos, gates, x2t, ln3, ys, ts)


def kernel(x, mem, w_in, conv_w, gn_conv, gn_sb, w_out, ln1_g, ln1_b, w_mq, w_mk, w_mv, w_mo,
           ln2_g, ln2_b, w_router_group, b_router_group, w_router_expert, b_router_expert,
           w_gate, w_up, w_down, ln3_g, ln3_b):
    for l in range(DEPTH):
        x = _layer(x, mem, w_in[l], conv_w[l], gn_conv[l], gn_sb[l], w_out[l], ln1_g[l], ln1_b[l],
                   w_mq[l], w_mk[l], w_mv[l], w_mo[l], ln2_g[l], ln2_b[l],
                   w_router_group[l], b_router_group[l], w_router_expert[l], b_router_expert[l],
                   w_gate[l], w_up[l], w_down[l], ln3_g[l], ln3_b[l])
    return x


def _layer(x, mem, w_in, conv_w, gn_conv, gn_sb, w_out, ln1_g, ln1_b, w_mq, w_mk, w_mv, w_mo,
           ln2_g, ln2_b, w_rg, b_rg, w_re, b_re, w_gate, w_up, w_down, ln3_g, ln3_b,
           ts=512, tq=512, tm=512, kv_nb=4):
    bn, sl, d = x.shape
    bf = lambda w: w.astype(BF16)
    yc, q, k, v = _proj_conv(x, bf(w_in), conv_w, gn_conv, ts)
    ysb = _sb_attention(q, k, v, gn_sb, tq)
    kb, vb = _kv_proj(mem, bf(w_mk), bf(w_mv), kv_nb)
    ln, wr_t, br = _router_params(ln1_g, ln1_b, ln2_g, ln2_b, w_rg, b_rg, w_re, b_re)
    x2t, route, wcnt = _post_mixer(x, yc, ysb, kb, vb, bf(w_out), bf(w_mq), bf(w_mo),
                                   ln, wr_t, br, ts)
    ln3 = jnp.stack([ln3_g, ln3_b]).astype(F32)
    out = _moe(x2t, route, wcnt, bf(w_gate), bf(w_up), bf(w_down), ln3, tm)
    return out.reshape(bn, sl, d)
```

```python
import functools

import jax
import jax.numpy as jnp
import numpy as np
from jax import lax
from jax.experimental import pallas as pl
from jax.experimental.pallas import tpu as pltpu

CONV_WIDTH = 512
CONV_GROUP = 64
CONV_K = 3
SB_HEADS = 8
SB_HEAD_DIM = 64
SB_WIDTH = SB_HEADS * SB_HEAD_DIM
MEM_HEADS = 4
N_GROUPS = 4
EXPERTS_PER_GROUP = 8
N_EXPERTS = N_GROUPS * EXPERTS_PER_GROUP
TOP_K = 2
DEPTH = 1
ALPHA = (2 * DEPTH) ** 0.25
LN_EPS = 1e-5
RMS_EPS = 1e-6

LANES = 128
SUBLANES = 8
VMEM_LIMIT = 56 * 1024 * 1024

BF16 = jnp.bfloat16
F32 = jnp.float32


def _split_bf16(a):
    hi = a.astype(BF16)
    lo = (a - hi.astype(F32)).astype(BF16)
    return hi, lo


def _dot(a, b):
    return jnp.dot(a, b, preferred_element_type=F32)


def _dot_nt(a, b):
    return lax.dot_general(a, b, (((1,), (1,)), ((), ())), preferred_element_type=F32)


def _layer_norm(r, g, b):
    mu = jnp.mean(r, axis=-1, keepdims=True)
    c = r - mu
    var = jnp.mean(c * c, axis=-1, keepdims=True)
    return c * lax.rsqrt(var + LN_EPS) * g + b


def _proj_conv_kernel(x_ref, w_ref, cw_ref, gn_ref, gmat_ref,
                      yc_ref, q_ref, k_ref, v_ref, ubuf_ref, *, ts):
    s = pl.program_id(1)
    xb = x_ref[0].astype(BF16)
    cw = CONV_WIDTH

    @pl.when(s == 0)
    def _():
        ubuf_ref[0:SUBLANES, :] = jnp.zeros((SUBLANES, cw), F32)

    cc = _dot(xb, w_ref[:, cw:2 * cw])
    ch = _dot(xb, w_ref[:, 2 * cw:3 * cw])
    u = cc * ch
    ubuf_ref[SUBLANES:SUBLANES + ts, :] = u
    um1 = ubuf_ref[SUBLANES - 1:SUBLANES - 1 + ts, :]
    um2 = ubuf_ref[SUBLANES - 2:SUBLANES - 2 + ts, :]
    conv = cw_ref[2:3, :] * u + cw_ref[1:2, :] * um1 + cw_ref[0:1, :] * um2
    ubuf_ref[0:SUBLANES, :] = ubuf_ref[ts:ts + SUBLANES, :]

    cb = _dot(xb, w_ref[:, 0:cw])
    y = cb * conv
    hi, lo = _split_bf16(y * y)
    gsum = _dot(hi, gmat_ref[...]) + _dot(lo, gmat_ref[...])
    yn = y * lax.rsqrt(gsum * (1.0 / CONV_GROUP) + RMS_EPS) * gn_ref[...]
    yc_ref[0] = yn.astype(yc_ref.dtype)

    sw = SB_WIDTH
    base = 3 * cw
    q_ref[0] = (_dot(xb, w_ref[:, base:base + sw]) * (SB_HEAD_DIM ** -0.5)).astype(q_ref.dtype)
    k_ref[0] = _dot(xb, w_ref[:, base + sw:base + 2 * sw]).astype(k_ref.dtype)
    v_ref[0] = _dot(xb, w_ref[:, base + 2 * sw:base + 3 * sw]).astype(v_ref.dtype)


def _proj_conv(x, w_in_b, conv_w, gn_conv, ts):
    bn, sl, d = x.shape
    gidx = np.arange(CONV_WIDTH) // CONV_GROUP
    gmat = jnp.asarray(gidx[:, None] == gidx[None, :], BF16)
    outs = [jax.ShapeDtypeStruct((bn, sl, CONV_WIDTH), BF16)] * 4
    tile = lambda w: pl.BlockSpec((1, ts, w), lambda b, s: (b, s, 0))
    full = lambda a: pl.BlockSpec(a.shape, lambda b, s: (0,) * a.ndim)
    gn = gn_conv.reshape(1, CONV_WIDTH)
    return pl.pallas_call(
        functools.partial(_proj_conv_kernel, ts=ts),
        grid=(bn, sl // ts),
        in_specs=[tile(d), full(w_in_b), full(conv_w), full(gn), full(gmat)],
        out_specs=[tile(CONV_WIDTH)] * 4,
        out_shape=outs,
        scratch_shapes=[pltpu.VMEM((ts + 2 * SUBLANES, CONV_WIDTH), F32)],
        compiler_params=pltpu.CompilerParams(
            dimension_semantics=("arbitrary", "arbitrary"), vmem_limit_bytes=VMEM_LIMIT),
        name="proj_conv",
    )(x, w_in_b, conv_w, gn, gmat)


SB_TK = 128
SB_UNROLL = 2
SB_DEAD = 192.0
LOG2E = 1.4426950408889634
BF16_BITS_OF_F32 = 0xFFFF0000


def _bits(a):
    return lax.bitcast_convert_type(a, jnp.uint32)


def _softplus2(z2):
    pos = jnp.maximum(z2, 0.0)
    return pos + jnp.log2(1.0 + jnp.exp2(z2 - 2.0 * pos))


def _sb_kernel(q_ref, k_ref, v_ref, gn_ref, cum_ref, o_ref, acc_ref, run_ref, *, tq):
    qi = pl.program_id(2)
    tk = SB_TK
    nd = tq // tk
    lane = lax.broadcasted_iota(jnp.int32, (1, LANES), 1)
    head0 = lane < SB_HEAD_DIM
    zero = jnp.zeros((), BF16)

    def per_head_rows(ref, kstart):
        blk = ref[0, pl.ds(kstart, tk), :]
        return jnp.concatenate([jnp.where(head0, blk, zero), jnp.where(head0, zero, blk)], axis=0)

    def tile(row0, kstart, diag, runs):
        rows = tq - row0
        z2 = _dot_nt(q_ref[0, row0:, :], per_head_rows(k_ref, kstart)) * LOG2E
        if diag:
            t_loc = lax.broadcasted_iota(jnp.int32, (tk, tk), 0)
            s_loc = lax.broadcasted_iota(jnp.int32, (tk, tk), 1)
            before = s_loc < t_loc
            causal = lambda m: jnp.concatenate(
                [jnp.where(before, m[:tk], 0.0)] + ([m[tk:]] if rows > tk else []), axis=0)
        a_parts, new_runs = [], []
        for h in range(2):
            zh = z2[:, h * tk:(h + 1) * tk]
            sp = _softplus2(zh)
            if diag:
                sp = causal(sp)
            hi = lax.bitcast_convert_type(_bits(sp) & jnp.uint32(BF16_BITS_OF_F32), F32)
            lhs = jnp.concatenate([hi.astype(BF16), (sp - hi).astype(BF16)], axis=1)
            cs = _dot(lhs, cum_ref[...])
            a = jnp.exp2(zh - cs[:, :tk] - runs[h])
            if diag:
                a = causal(a)
            a_parts.append(a.astype(BF16))
            new_runs.append(runs[h] + cs[:, tk:])
        return jnp.concatenate(a_parts, axis=1), new_runs

    acc = None
    runs = None
    for d in reversed(range(nd)):
        row0 = d * tk
        kstart = pl.multiple_of(qi * tq + row0, tk)
        fresh = jnp.zeros((tk, LANES), F32)
        runs = [fresh, fresh] if runs is None else [jnp.concatenate([fresh, r], axis=0) for r in runs]
        a, runs = tile(row0, kstart, True, runs)
        av = _dot(a, per_head_rows(v_ref, kstart))
        acc = av if acc is None else av + jnp.concatenate([fresh, acc], axis=0)
    acc_ref[...] = acc
    run_ref[0] = runs[0]
    run_ref[1] = runs[1]

    n_iter = qi * nd // SB_UNROLL

    def more(carry):
        j, dead = carry
        return jnp.logical_and(j < n_iter, dead == 0)

    def body(carry):
        j, _ = carry
        runs = [run_ref[0], run_ref[1]]
        a_parts, v_parts = [], []
        for u in range(SB_UNROLL):
            kstart = pl.multiple_of((qi * nd - 1 - (j * SB_UNROLL + u)) * tk, tk)
            a, runs = tile(0, kstart, False, runs)
            a_parts.append(a)
            v_parts.append(per_head_rows(v_ref, kstart))
        acc_ref[...] += _dot(jnp.concatenate(a_parts, axis=1), jnp.concatenate(v_parts, axis=0))
        run_ref[0] = runs[0]
        run_ref[1] = runs[1]
        dead = jnp.min(jnp.minimum(runs[0], runs[1])) > SB_DEAD
        return j + 1, dead.astype(jnp.int32)

    lax.while_loop(more, body, (jnp.int32(0), jnp.int32(0)))

    y = acc_ref[...]
    y2 = y * y
    s0 = jnp.sum(jnp.where(head0, y2, 0.0), axis=-1, keepdims=True)
    s1 = jnp.sum(jnp.where(head0, 0.0, y2), axis=-1, keepdims=True)
    ms = jnp.where(head0, s0, s1) * (1.0 / SB_HEAD_DIM)
    o_ref[0] = (y * lax.rsqrt(ms + RMS_EPS) * gn_ref[...]).astype(o_ref.dtype)


def _sb_attention(q, k, v, gn_sb, tq):
    bn, sl, w = q.shape
    npair = w // LANES
    j = np.arange(SB_TK)
    assert (tq // SB_TK) % SB_UNROLL == 0
    cum = np.concatenate([(j[:, None] >= j[None, :]), np.ones((SB_TK, SB_TK), bool)], axis=1)
    cum = jnp.asarray(np.concatenate([cum, cum], axis=0), BF16)
    gn = gn_sb.reshape(1, w)
    return pl.pallas_call(
        functools.partial(_sb_kernel, tq=tq),
        grid=(bn, npair, sl // tq),
        in_specs=[
            pl.BlockSpec((1, tq, LANES), lambda b, p, i: (b, i, p)),
            pl.BlockSpec((1, sl, LANES), lambda b, p, i: (b, 0, p)),
            pl.BlockSpec((1, sl, LANES), lambda b, p, i: (b, 0, p)),
            pl.BlockSpec((1, LANES), lambda b, p, i: (0, p)),
            pl.BlockSpec(cum.shape, lambda b, p, i: (0, 0)),
        ],
        out_specs=pl.BlockSpec((1, tq, LANES), lambda b, p, i: (b, i, p)),
        out_shape=jax.ShapeDtypeStruct((bn, sl, w), BF16),
        scratch_shapes=[pltpu.VMEM((tq, LANES), F32), pltpu.VMEM((2, tq, LANES), F32)],
        compiler_params=pltpu.CompilerParams(
            dimension_semantics=("arbitrary", "arbitrary", "arbitrary"),
            vmem_limit_bytes=VMEM_LIMIT),
        name="sb_attention",
    )(q, k, v, gn, cum)


def _kv_kernel(mem_ref, wk_ref, wv_ref, k_ref, v_ref):
    nb, ml, d = mem_ref.shape
    m = mem_ref[...].reshape(nb * ml, d).astype(BF16)
    k_ref[...] = _dot(m, wk_ref[...]).reshape(nb, ml, d).astype(k_ref.dtype)
    v_ref[...] = _dot(m, wv_ref[...]).reshape(nb, ml, d).astype(v_ref.dtype)


def _kv_proj(mem, wk_b, wv_b, nb):
    bn, ml, d = mem.shape
    blk = pl.BlockSpec((nb, ml, d), lambda i: (i, 0, 0))
    full = pl.BlockSpec((d, d), lambda i: (0, 0))
    return pl.pallas_call(
        _kv_kernel,
        grid=(bn // nb,),
        in_specs=[blk, full, full],
        out_specs=[blk, blk],
        out_shape=[jax.ShapeDtypeStruct((bn, ml, d), BF16)] * 2,
        compiler_params=pltpu.CompilerParams(
            dimension_semantics=("arbitrary",), vmem_limit_bytes=VMEM_LIMIT),
        name="kv_proj",
    )(mem, wk_b, wv_b)


ROUTE_ROWS = 8
ROUTER_ROWS = 40


def _to_token_tiles(ref, val, ts):
    for c in range(val.shape[1] // LANES):
        ref[pl.ds(c, ts, stride=SUBLANES), :] = val[:, c * LANES:(c + 1) * LANES]


def _from_token_tiles(ref, ts, d):
    return jnp.concatenate(
        [ref[pl.ds(c, ts, stride=SUBLANES), :] for c in range(d // LANES)], axis=1)


def _post_kernel(x_ref, yc_ref, ysb_ref, kb_ref, vb_ref, wo_ref, wq_ref, wmo_ref,
                 ln_ref, wr_ref, br_ref, tri_ref, lt_ref,
                 x2_ref, route_ref, cnt_ref, *, ts):
    d = x_ref.shape[2]
    cw = yc_ref.shape[2]
    x = x_ref[0]
    h1 = _dot(yc_ref[0], wo_ref[0:cw, :]) + _dot(ysb_ref[0], wo_ref[cw:, :])
    x1 = _layer_norm(ALPHA * x + h1, ln_ref[0:1, :], ln_ref[1:2, :])

    hd = d // MEM_HEADS
    q = (_dot(x1.astype(BF16), wq_ref[...]) * (hd ** -0.5)).astype(BF16)
    outs = []
    for h in range(MEM_HEADS):
        sc = _dot_nt(q[:, h * hd:(h + 1) * hd], kb_ref[0, :, h * hd:(h + 1) * hd])
        p = jnp.exp(sc - jnp.max(sc, axis=-1, keepdims=True))
        den = jnp.sum(p, axis=-1, keepdims=True)
        o = _dot(p.astype(BF16), vb_ref[0, :, h * hd:(h + 1) * hd])
        outs.append((o * (1.0 / den)).astype(BF16))
    o = jnp.concatenate(outs, axis=1)
    h2 = _dot(o, wmo_ref[...])
    x2 = _layer_norm(ALPHA * x1 + h2, ln_ref[2:3, :], ln_ref[3:4, :])
    _to_token_tiles(x2_ref, x2, ts)

    lt = _dot_nt(wr_ref[...], x2.astype(BF16)) + br_ref[:, 0:1]
    gl = [lt[g:g + 1, :] for g in range(N_GROUPS)]
    gmax = functools.reduce(jnp.maximum, gl)
    gden = functools.reduce(lambda a, b: a + b, [jnp.exp(l - gmax) for l in gl])
    gprob = 1.0 / gden
    grp = jnp.full(gmax.shape, N_GROUPS - 1, jnp.int32)
    for g in reversed(range(N_GROUPS - 1)):
        grp = jnp.where(gl[g] >= gmax, g, grp)
    el = lt[N_GROUPS:N_GROUPS + EXPERTS_PER_GROUP, :]
    for g in range(1, N_GROUPS):
        lo = N_GROUPS + g * EXPERTS_PER_GROUP
        el = jnp.where(grp == g, lt[lo:lo + EXPERTS_PER_GROUP, :], el)
    row = lax.broadcasted_iota(jnp.int32, el.shape, 0)
    t1 = jnp.max(el, axis=0, keepdims=True)
    i1 = jnp.min(jnp.where(el >= t1, row, EXPERTS_PER_GROUP), axis=0, keepdims=True)
    el2 = jnp.where(row == i1, -jnp.inf, el)
    t2 = jnp.max(el2, axis=0, keepdims=True)
    i2 = jnp.min(jnp.where(el2 >= t2, row, EXPERTS_PER_GROUP), axis=0, keepdims=True)
    e2w = jnp.exp(t2 - t1)
    g0 = gprob / (1.0 + e2w)
    g1 = gprob * e2w / (1.0 + e2w)
    e0 = grp * EXPERTS_PER_GROUP + i1
    e1 = grp * EXPERTS_PER_GROUP + i2

    erow = lax.broadcasted_iota(jnp.int32, (N_EXPERTS, ts), 0)
    oh0 = erow == e0
    oh1 = erow == e1
    both = jnp.where(jnp.logical_or(oh0, oh1), 1.0, 0.0).astype(BF16)
    tot = _dot(both, tri_ref[...])
    lower = _dot(lt_ref[...], both).astype(BF16)
    offs = _dot(lower, tri_ref[:, ts:])
    pos = tot[:, :ts] + offs[:, 0:1]
    p0 = jnp.sum(jnp.where(oh0, pos, 0.0), axis=0, keepdims=True)
    p1 = jnp.sum(jnp.where(oh1, pos, 0.0), axis=0, keepdims=True)
    cnt_ref[0] = tot[:, ts:]

    zero = jnp.zeros_like(g0)
    route_ref[0] = jnp.concatenate(
        [e0.astype(F32), e1.astype(F32), p0, p1, g0, g1, zero, zero], axis=0)


def _post_mixer(x, yc, ysb, kb, vb, wo_b, wq_b, wmo_b, ln, wr_t, br, ts):
    bn, sl, d = x.shape
    cw = yc.shape[2]
    ml = kb.shape[1]
    nt = sl // ts
    j = np.arange(ts)
    tri = jnp.asarray(np.concatenate([j[:, None] < j[None, :], np.ones((ts, LANES), bool)], axis=1), BF16)
    je = np.arange(N_EXPERTS)
    lt = jnp.asarray(je[None, :] < je[:, None], BF16)
    tile = lambda w: pl.BlockSpec((1, ts, w), lambda b, s: (b, s, 0))
    full = lambda a: pl.BlockSpec(a.shape, lambda b, s: (0,) * a.ndim)
    kvspec = pl.BlockSpec((1, ml, d), lambda b, s: (b, 0, 0))
    return pl.pallas_call(
        functools.partial(_post_kernel, ts=ts),
        grid=(bn, nt),
        in_specs=[tile(d), tile(cw), tile(ysb.shape[2]), kvspec, kvspec,
                  full(wo_b), full(wq_b), full(wmo_b), full(ln), full(wr_t), full(br), full(tri),
                  full(lt)],
        out_specs=[
            pl.BlockSpec((ts * SUBLANES, LANES), lambda b, s: (b * nt + s, 0)),
            pl.BlockSpec((1, ROUTE_ROWS, ts), lambda b, s: (b * nt + s, 0, 0)),
            pl.BlockSpec((1, N_EXPERTS, LANES), lambda b, s: (b * nt + s, 0, 0)),
        ],
        out_shape=[
            jax.ShapeDtypeStruct((bn * sl * SUBLANES, LANES), F32),
            jax.ShapeDtypeStruct((bn * nt, ROUTE_ROWS, ts), F32),
            jax.ShapeDtypeStruct((bn * nt, N_EXPERTS, LANES), F32),
        ],
        compiler_params=pltpu.CompilerParams(
            dimension_semantics=("arbitrary", "arbitrary"), vmem_limit_bytes=VMEM_LIMIT),
        name="post_mixer",
    )(x, yc, ysb, kb, vb, wo_b, wq_b, wmo_b, ln, wr_t, br, tri, lt)


def _router_params(ln1_g, ln1_b, ln2_g, ln2_b, w_rg, b_rg, w_re, b_re):
    d = w_rg.shape[0]
    ln = jnp.stack([ln1_g, ln1_b, ln2_g, ln2_b]).astype(F32)
    pad = ROUTER_ROWS - N_GROUPS - N_EXPERTS
    wr_t = jnp.concatenate([w_rg.T, w_re.T, jnp.zeros((pad, d), w_rg.dtype)], axis=0).astype(BF16)
    br = jnp.concatenate([b_rg, b_re, jnp.zeros((pad,), b_rg.dtype)]).astype(F32)
    br = jnp.broadcast_to(br[:, None], (ROUTER_ROWS, LANES))
    return ln, wr_t, br


CHUNK_LOW_BITS = 6


def _chunk_copies(plan_ref, w, hbm_ref, buf_ref, sem_of, to_hbm, nbits, act):
    def per_expert(e, c):
        off = plan_ref[w, e]
        cnt = plan_ref[w, N_EXPERTS + e]
        dst = plan_ref[w, 2 * N_EXPERTS + e]

        def piece(b):
            @pl.when(((cnt >> b) & 1) == 1)
            def _():
                done = (cnt >> (b + 1)) << (b + 1)
                n = (1 << b) * SUBLANES
                loc = buf_ref.at[pl.ds(pl.multiple_of((off + done) * SUBLANES, SUBLANES), n), :]
                glob = hbm_ref.at[pl.ds(pl.multiple_of((dst + done) * SUBLANES, SUBLANES), n), :]
                cp = (pltpu.make_async_copy(loc, glob, sem_of(b)) if to_hbm
                      else pltpu.make_async_copy(glob, loc, sem_of(b)))
                act(cp)

        low = min(CHUNK_LOW_BITS, nbits)
        for b in range(low):
            piece(b)

        @pl.when((cnt >> low) != 0)
        def _():
            for b in range(low, nbits):
                piece(b)

        return c

    lax.fori_loop(0, N_EXPERTS, per_expert, 0)


TOKEN_UNROLL = 8


def _dispatch_kernel(plan_ref, nu_ref, pos_ref, x2_ref, xs_ref, sbuf_ref, sems, *, td, tm, nbits):
    w = pl.program_id(0)
    last = pl.num_programs(0) - 1
    slot = w % 2
    scur_ref = sbuf_ref.at[slot]

    def send(win, act):
        s = win % 2
        _chunk_copies(plan_ref, win, xs_ref, sbuf_ref.at[s], lambda b: sems.at[s, b], True, nbits, act)

    @pl.when(w == 0)
    def _():
        block = tm * SUBLANES
        zeros_ref = sbuf_ref.at[1]
        zeros_ref[0:block, :] = jnp.zeros((block, LANES), F32)
        pad_row = pl.num_programs(0)
        pads = lambda act: _chunk_copies(
            plan_ref, pad_row, xs_ref, zeros_ref, lambda b: sems.at[1, b], True, nbits, act)
        pads(lambda cp: cp.start())
        pads(lambda cp: cp.wait())
        tail = lambda i: pltpu.make_async_copy(
            zeros_ref.at[0:block, :], xs_ref.at[pl.ds(pl.multiple_of(i * block, block), block), :],
            sems.at[1, 0])
        n_blocks = xs_ref.shape[0] // block
        lax.fori_loop(nu_ref[0], n_blocks, lambda i, c: (tail(i).start(), c)[1], 0)
        lax.fori_loop(nu_ref[0], n_blocks, lambda i, c: (tail(i).wait(), c)[1], 0)

    def place(i, c):
        for u in range(TOKEN_UNROLL):
            t = i * TOKEN_UNROLL + u
            tok = x2_ref[pl.ds(pl.multiple_of(t * SUBLANES, SUBLANES), SUBLANES), :]
            for k in range(TOP_K):
                p = pos_ref[0, 0, k * td + t]
                scur_ref[pl.ds(pl.multiple_of(p * SUBLANES, SUBLANES), SUBLANES), :] = tok
        return c

    lax.fori_loop(0, td // TOKEN_UNROLL, place, 0)
    send(w, lambda cp: cp.start())

    @pl.when(w > 0)
    def _():
        send(w - 1, lambda cp: cp.wait())

    @pl.when(w == last)
    def _():
        send(w, lambda cp: cp.wait())


def _dispatch(plan, n_used, pos, x2t, n_rows, td, tm):
    nt = pos.shape[0]
    nbits = td.bit_length()
    assert tm <= TOP_K * td and n_rows % tm == 0
    return pl.pallas_call(
        functools.partial(_dispatch_kernel, td=td, tm=tm, nbits=nbits),
        grid_spec=pltpu.PrefetchScalarGridSpec(
            num_scalar_prefetch=2,
            grid=(nt,),
            in_specs=[
                pl.BlockSpec((1, 1, TOP_K * td), lambda i, plan, nu: (i, 0, 0), memory_space=pltpu.SMEM),
                pl.BlockSpec((td * SUBLANES, LANES), lambda i, plan, nu: (i, 0)),
            ],
            out_specs=pl.BlockSpec(memory_space=pl.ANY),
            scratch_shapes=[pltpu.VMEM((2, TOP_K * td * SUBLANES, LANES), F32),
                            pltpu.SemaphoreType.DMA((2, nbits))],
        ),
        out_shape=jax.ShapeDtypeStruct((n_rows * SUBLANES, LANES), F32),
        compiler_params=pltpu.CompilerParams(
            dimension_semantics=("arbitrary",), vmem_limit_bytes=VMEM_LIMIT),
        name="dispatch",
    )(plan, n_used, pos, x2t)


def _expert_kernel(be_ref, nu_ref, xs_ref, wg_ref, wu_ref, wd_ref, ys_ref, *, tm):
    del be_ref
    d = wg_ref.shape[1]

    @pl.when(pl.program_id(0) < nu_ref[0])
    def _():
        xb = _from_token_tiles(xs_ref, tm, d).astype(BF16)
        g = _dot(xb, wg_ref[0])
        u = _dot(xb, wu_ref[0])
        h = g * jax.nn.sigmoid(g) * u
        _to_token_tiles(ys_ref, _dot(h.astype(BF16), wd_ref[0]), tm)

    @pl.when(pl.program_id(0) >= nu_ref[0])
    def _():
        ys_ref[...] = jnp.zeros(ys_ref.shape, ys_ref.dtype)


def _experts(block_expert, n_used, xs, wg_b, wu_b, wd_b, tm):
    ne, d, f = wg_b.shape
    nb = xs.shape[0] // (tm * SUBLANES)
    rows = lambda i, be, nu: (i, 0)
    wsel = lambda i, be, nu: (be[i], 0, 0)
    return pl.pallas_call(
        functools.partial(_expert_kernel, tm=tm),
        grid_spec=pltpu.PrefetchScalarGridSpec(
            num_scalar_prefetch=2,
            grid=(nb,),
            in_specs=[
                pl.BlockSpec((tm * SUBLANES, LANES), rows),
                pl.BlockSpec((1, d, f), wsel),
                pl.BlockSpec((1, d, f), wsel),
                pl.BlockSpec((1, f, d), wsel),
            ],
            out_specs=pl.BlockSpec((tm * SUBLANES, LANES), rows),
        ),
        out_shape=jax.ShapeDtypeStruct(xs.shape, F32),
        compiler_params=pltpu.CompilerParams(
            dimension_semantics=("arbitrary",), vmem_limit_bytes=VMEM_LIMIT),
        name="experts",
    )(block_expert, n_used, xs, wg_b, wu_b, wd_b)


def _combine_kernel(plan_ref, pos_ref, gate_ref, x2_ref, ln_ref, ys_ref, o_ref,
                    ybuf_ref, rbuf_ref, sems, *, tc, nbits):
    w = pl.program_id(0)
    d = o_ref.shape[1]
    slot = w % 2

    def fetch(win, act):
        s = win % 2
        _chunk_copies(plan_ref, win, ys_ref, ybuf_ref.at[s], lambda b: sems.at[s, b], False, nbits, act)

    @pl.when(w == 0)
    def _():
        fetch(w, lambda cp: cp.start())

    @pl.when(w + 1 < pl.num_programs(0))
    def _():
        fetch(w + 1, lambda cp: cp.start())

    fetch(w, lambda cp: cp.wait())
    ycur_ref = ybuf_ref.at[slot]

    def gather(i, c):
        for u in range(TOKEN_UNROLL):
            t = i * TOKEN_UNROLL + u
            rows = pl.ds(pl.multiple_of(t * SUBLANES, SUBLANES), SUBLANES)
            moe = None
            for k in range(TOP_K):
                p = pos_ref[0, 0, k * tc + t]
                y = ycur_ref[pl.ds(pl.multiple_of(p * SUBLANES, SUBLANES), SUBLANES), :]
                term = gate_ref[0, 0, k * tc + t] * y
                moe = term if moe is None else moe + term
            rbuf_ref[rows, :] = ALPHA * x2_ref[rows, :] + moe
        return c

    lax.fori_loop(0, tc // TOKEN_UNROLL, gather, 0)
    o_ref[...] = _layer_norm(_from_token_tiles(rbuf_ref, tc, d), ln_ref[0:1, :], ln_ref[1:2, :])


def _combine(plan, pos, gates, x2t, ln3, ys, tc):
    nt = pos.shape[0]
    d = ln3.shape[1]
    nbits = tc.bit_length()
    smem_tile = pl.BlockSpec((1, 1, TOP_K * tc), lambda i, plan: (i, 0, 0), memory_space=pltpu.SMEM)
    return pl.pallas_call(
        functools.partial(_combine_kernel, tc=tc, nbits=nbits),
        grid_spec=pltpu.PrefetchScalarGridSpec(
            num_scalar_prefetch=1,
            grid=(nt,),
            in_specs=[
                smem_tile,
                smem_tile,
                pl.BlockSpec((tc * SUBLANES, LANES), lambda i, plan: (i, 0)),
                pl.BlockSpec(ln3.shape, lambda i, plan: (0, 0)),
                pl.BlockSpec(memory_space=pl.ANY),
            ],
            out_specs=pl.BlockSpec((tc, d), lambda i, plan: (i, 0)),
            scratch_shapes=[pltpu.VMEM((2, TOP_K * tc * SUBLANES, LANES), F32),
                            pltpu.VMEM((tc * SUBLANES, LANES), F32),
                            pltpu.SemaphoreType.DMA((2, nbits))],
        ),
        out_shape=jax.ShapeDtypeStruct((nt * tc, d), F32),
        compiler_params=pltpu.CompilerParams(
            dimension_semantics=("arbitrary",), vmem_limit_bytes=VMEM_LIMIT),
        name="combine",
    )(plan, pos, gates, x2t, ln3, ys)


def _routing_plan(route, wcnt, tm):
    nt, _, ts = route.shape
    cw = wcnt[:, :, 0].astype(jnp.int32)
    cnt = jnp.sum(cw, axis=0)
    padded = (cnt + tm - 1) // tm * tm
    pend = jnp.cumsum(padded)
    pstart = pend - padded
    base = jnp.cumsum(cw, axis=0) - cw
    offs = jnp.cumsum(cw, axis=1) - cw
    plan = jnp.concatenate([offs, cw, pstart[None, :] + base], axis=1)
    pad_plan = jnp.concatenate([jnp.zeros_like(cnt), padded - cnt, pstart + cnt])
    plan = jnp.concatenate([plan, pad_plan[None, :]], axis=0)
    pos = route[:, 2:4, :].astype(jnp.int32).reshape(nt, 1, TOP_K * ts)
    gates = route[:, 4:6, :].reshape(nt, 1, TOP_K * ts)
    nb = nt * ts * TOP_K // tm + N_EXPERTS
    first_row = jnp.arange(nb, dtype=jnp.int32) * tm
    block_expert = jnp.minimum(
        jnp.sum((pend[None, :] <= first_row[:, None]).astype(jnp.int32), axis=1), N_EXPERTS - 1)
    n_used = (pend[-1:] // tm).astype(jnp.int32)
    return plan, pos, gates, block_expert, n_used, nb


def _moe(x2t, route, wcnt, wg_b, wu_b, wd_b, ln3, tm):
    ts = route.shape[2]
    plan, pos, gates, block_expert, n_used, nb = _routing_plan(route, wcnt, tm)
    xs = _dispatch(plan, n_used, pos, x2t, nb * tm, ts, tm)
    ys = _experts(block_expert, n_used, xs, wg_b, wu_b, wd_b, tm)
    return _combine(plan, pos, gates, x2t, ln3, ys, ts)


def kernel(x, mem, w_in, conv_w, gn_conv, gn_sb, w_out, ln1_g, ln1_b, w_mq, w_mk, w_mv, w_mo,
           ln2_g, ln2_b, w_router_group, b_router_group, w_router_expert, b_router_expert,
           w_gate, w_up, w_down, ln3_g, ln3_b):
    for l in range(DEPTH):
        x = _layer(x, mem, w_in[l], conv_w[l], gn_conv[l], gn_sb[l], w_out[l], ln1_g[l], ln1_b[l],
                   w_mq[l], w_mk[l], w_mv[l], w_mo[l], ln2_g[l], ln2_b[l],
                   w_router_group[l], b_router_group[l], w_router_expert[l], b_router_expert[l],
                   w_gate[l], w_up[l], w_down[l], ln3_g[l], ln3_b[l])
    return x


def _layer(x, mem, w_in, conv_w, gn_conv, gn_sb, w_out, ln1_g, ln1_b, w_mq, w_mk, w_mv, w_mo,
           ln2_g, ln2_b, w_rg, b_rg, w_re, b_re, w_gate, w_up, w_down, ln3_g, ln3_b,
           ts=512, tq=512, tm=512, kv_nb=4):
    bn, sl, d = x.shape
    bf = lambda w: w.astype(BF16)
    yc, q, k, v = _proj_conv(x, bf(w_in), conv_w, gn_conv, ts)
    ysb = _sb_attention(q, k, v, gn_sb, tq)
    kb, vb = _kv_proj(mem, bf(w_mk), bf(w_mv), kv_nb)
    ln, wr_t, br = _router_params(ln1_g, ln1_b, ln2_g, ln2_b, w_rg, b_rg, w_re, b_re)
    x2t, route, wcnt = _post_mixer(x, yc, ysb, kb, vb, bf(w_out), bf(w_mq), bf(w_mo),
                                   ln, wr_t, br, ts)
    ln3 = jnp.stack([ln3_g, ln3_b]).astype(F32)
    out = _moe(x2t, route, wcnt, bf(w_gate), bf(w_up), bf(w_down), ln3, tm)
    return out.reshape(bn, sl, d)
```

```python
import functools

import jax
import jax.numpy as jnp
import numpy as np
from jax import lax
from jax.experimental import pallas as pl
from jax.experimental.pallas import tpu as pltpu

CONV_WIDTH = 512
CONV_GROUP = 64
CONV_K = 3
SB_HEADS = 8
SB_HEAD_DIM = 64
SB_WIDTH = SB_HEADS * SB_HEAD_DIM
MEM_HEADS = 4
N_GROUPS = 4
EXPERTS_PER_GROUP = 8
N_EXPERTS = N_GROUPS * EXPERTS_PER_GROUP
TOP_K = 2
DEPTH = 1
ALPHA = (2 * DEPTH) ** 0.25
LN_EPS = 1e-5
RMS_EPS = 1e-6

LANES = 128
SUBLANES = 8
VMEM_LIMIT = 56 * 1024 * 1024

BF16 = jnp.bfloat16
F32 = jnp.float32


def _split_bf16(a):
    hi = a.astype(BF16)
    lo = (a - hi.astype(F32)).astype(BF16)
    return hi, lo


def _dot(a, b):
    return jnp.dot(a, b, preferred_element_type=F32)


def _dot_nt(a, b):
    return lax.dot_general(a, b, (((1,), (1,)), ((), ())), preferred_element_type=F32)


def _layer_norm(r, g, b):
    mu = jnp.mean(r, axis=-1, keepdims=True)
    c = r - mu
    var = jnp.mean(c * c, axis=-1, keepdims=True)
    return c * lax.rsqrt(var + LN_EPS) * g + b


def _proj_conv_kernel(x_ref, w_ref, cw_ref, gn_ref, gmat_ref,
                      yc_ref, q_ref, k_ref, v_ref, ubuf_ref, *, ts):
    s = pl.program_id(1)
    xb = x_ref[0].astype(BF16)
    cw = CONV_WIDTH

    @pl.when(s == 0)
    def _():
        ubuf_ref[0:SUBLANES, :] = jnp.zeros((SUBLANES, cw), F32)

    cc = _dot(xb, w_ref[:, cw:2 * cw])
    ch = _dot(xb, w_ref[:, 2 * cw:3 * cw])
    u = cc * ch
    ubuf_ref[SUBLANES:SUBLANES + ts, :] = u
    um1 = ubuf_ref[SUBLANES - 1:SUBLANES - 1 + ts, :]
    um2 = ubuf_ref[SUBLANES - 2:SUBLANES - 2 + ts, :]
    conv = cw_ref[2:3, :] * u + cw_ref[1:2, :] * um1 + cw_ref[0:1, :] * um2
    ubuf_ref[0:SUBLANES, :] = ubuf_ref[ts:ts + SUBLANES, :]

    cb = _dot(xb, w_ref[:, 0:cw])
    y = cb * conv
    hi, lo = _split_bf16(y * y)
    gsum = _dot(hi, gmat_ref[...]) + _dot(lo, gmat_ref[...])
    yn = y * lax.rsqrt(gsum * (1.0 / CONV_GROUP) + RMS_EPS) * gn_ref[...]
    yc_ref[0] = yn.astype(yc_ref.dtype)

    sw = SB_WIDTH
    base = 3 * cw
    q_ref[0] = (_dot(xb, w_ref[:, base:base + sw]) * (SB_HEAD_DIM ** -0.5)).astype(q_ref.dtype)
    k_ref[0] = _dot(xb, w_ref[:, base + sw:base + 2 * sw]).astype(k_ref.dtype)
    v_ref[0] = _dot(xb, w_ref[:, base + 2 * sw:base + 3 * sw]).astype(v_ref.dtype)


def _proj_conv(x, w_in_b, conv_w, gn_conv, ts):
    bn, sl, d = x.shape
    gidx = np.arange(CONV_WIDTH) // CONV_GROUP
    gmat = jnp.asarray(gidx[:, None] == gidx[None, :], BF16)
    outs = [jax.ShapeDtypeStruct((bn, sl, CONV_WIDTH), BF16)] * 4
    tile = lambda w: pl.BlockSpec((1, ts, w), lambda b, s: (b, s, 0))
    full = lambda a: pl.BlockSpec(a.shape, lambda b, s: (0,) * a.ndim)
    gn = gn_conv.reshape(1, CONV_WIDTH)
    return pl.pallas_call(
        functools.partial(_proj_conv_kernel, ts=ts),
        grid=(bn, sl // ts),
        in_specs=[tile(d), full(w_in_b), full(conv_w), full(gn), full(gmat)],
        out_specs=[tile(CONV_WIDTH)] * 4,
        out_shape=outs,
        scratch_shapes=[pltpu.VMEM((ts + 2 * SUBLANES, CONV_WIDTH), F32)],
        compiler_params=pltpu.CompilerParams(
            dimension_semantics=("arbitrary", "arbitrary"), vmem_limit_bytes=VMEM_LIMIT),
        name="proj_conv",
    )(x, w_in_b, conv_w, gn, gmat)


SB_TK = 128
SB_UNROLL = 2
SB_DEAD = 192.0
LOG2E = 1.4426950408889634
BF16_BITS_OF_F32 = 0xFFFF0000


def _bits(a):
    return lax.bitcast_convert_type(a, jnp.uint32)


def _softplus2(z2):
    pos = jnp.maximum(z2, 0.0)
    return pos + jnp.log2(1.0 + jnp.exp2(z2 - 2.0 * pos))


def _sb_kernel(q_ref, k_ref, v_ref, gn_ref, cum_ref, o_ref, acc_ref, run_ref, *, tq):
    qi = pl.program_id(2)
    tk = SB_TK
    nd = tq // tk
    lane = lax.broadcasted_iota(jnp.int32, (1, LANES), 1)
    head0 = lane < SB_HEAD_DIM
    zero = jnp.zeros((), BF16)

    def per_head_rows(ref, kstart):
        blk = ref[0, pl.ds(kstart, tk), :]
        return jnp.concatenate([jnp.where(head0, blk, zero), jnp.where(head0, zero, blk)], axis=0)

    def tile(row0, kstart, diag, runs):
        rows = tq - row0
        z2 = _dot_nt(q_ref[0, row0:, :], per_head_rows(k_ref, kstart)) * LOG2E
        if diag:
            t_loc = lax.broadcasted_iota(jnp.int32, (tk, tk), 0)
            s_loc = lax.broadcasted_iota(jnp.int32, (tk, tk), 1)
            before = s_loc < t_loc
            causal = lambda m: jnp.concatenate(
                [jnp.where(before, m[:tk], 0.0)] + ([m[tk:]] if rows > tk else []), axis=0)
        a_parts, new_runs = [], []
        for h in range(2):
            zh = z2[:, h * tk:(h + 1) * tk]
            sp = _softplus2(zh)
            if diag:
                sp = causal(sp)
            hi = lax.bitcast_convert_type(_bits(sp) & jnp.uint32(BF16_BITS_OF_F32), F32)
            lhs = jnp.concatenate([hi.astype(BF16), (sp - hi).astype(BF16)], axis=1)
            cs = _dot(lhs, cum_ref[...])
            a = jnp.exp2(zh - cs[:, :tk] - runs[h])
            if diag:
                a = causal(a)
            a_parts.append(a.astype(BF16))
            new_runs.append(runs[h] + cs[:, tk:])
        return jnp.concatenate(a_parts, axis=1), new_runs

    acc = None
    runs = None
    for d in reversed(range(nd)):
        row0 = d * tk
        kstart = pl.multiple_of(qi * tq + row0, tk)
        fresh = jnp.zeros((tk, LANES), F32)
        runs = [fresh, fresh] if runs is None else [jnp.concatenate([fresh, r], axis=0) for r in runs]
        a, runs = tile(row0, kstart, True, runs)
        av = _dot(a, per_head_rows(v_ref, kstart))
        acc = av if acc is None else av + jnp.concatenate([fresh, acc], axis=0)
    acc_ref[...] = acc
    run_ref[0] = runs[0]
    run_ref[1] = runs[1]

    n_iter = qi * nd // SB_UNROLL

    def more(carry):
        j, dead = carry
        return jnp.logical_and(j < n_iter, dead == 0)

    def body(carry):
        j, _ = carry
        runs = [run_ref[0], run_ref[1]]
        a_parts, v_parts = [], []
        for u in range(SB_UNROLL):
            kstart = pl.multiple_of((qi * nd - 1 - (j * SB_UNROLL + u)) * tk, tk)
            a, runs = tile(0, kstart, False, runs)
            a_parts.append(a)
            v_parts.append(per_head_rows(v_ref, kstart))
        acc_ref[...] += _dot(jnp.concatenate(a_parts, axis=1), jnp.concatenate(v_parts, axis=0))
        run_ref[0] = runs[0]
        run_ref[1] = runs[1]
        dead = jnp.min(jnp.minimum(runs[0], runs[1])) > SB_DEAD
        return j + 1, dead.astype(jnp.int32)

    lax.while_loop(more, body, (jnp.int32(0), jnp.int32(0)))

    y = acc_ref[...]
    y2 = y * y
    s0 = jnp.sum(jnp.where(head0, y2, 0.0), axis=-1, keepdims=True)
    s1 = jnp.sum(jnp.where(head0, 0.0, y2), axis=-1, keepdims=True)
    ms = jnp.where(head0, s0, s1) * (1.0 / SB_HEAD_DIM)
    o_ref[0] = (y * lax.rsqrt(ms + RMS_EPS) * gn_ref[...]).astype(o_ref.dtype)


def _sb_attention(q, k, v, gn_sb, tq):
    bn, sl, w = q.shape
    npair = w // LANES
    j = np.arange(SB_TK)
    assert (tq // SB_TK) % SB_UNROLL == 0
    cum = np.concatenate([(j[:, None] >= j[None, :]), np.ones((SB_TK, SB_TK), bool)], axis=1)
    cum = jnp.asarray(np.concatenate([cum, cum], axis=0), BF16)
    gn = gn_sb.reshape(1, w)
    return pl.pallas_call(
        functools.partial(_sb_kernel, tq=tq),
        grid=(bn, npair, sl // tq),
        in_specs=[
            pl.BlockSpec((1, tq, LANES), lambda b, p, i: (b, i, p)),
            pl.BlockSpec((1, sl, LANES), lambda b, p, i: (b, 0, p)),
            pl.BlockSpec((1, sl, LANES), lambda b, p, i: (b, 0, p)),
            pl.BlockSpec((1, LANES), lambda b, p, i: (0, p)),
            pl.BlockSpec(cum.shape, lambda b, p, i: (0, 0)),
        ],
        out_specs=pl.BlockSpec((1, tq, LANES), lambda b, p, i: (b, i, p)),
        out_shape=jax.ShapeDtypeStruct((bn, sl, w), BF16),
        scratch_shapes=[pltpu.VMEM((tq, LANES), F32), pltpu.VMEM((2, tq, LANES), F32)],
        compiler_params=pltpu.CompilerParams(
            dimension_semantics=("arbitrary", "arbitrary", "arbitrary"),
            vmem_limit_bytes=VMEM_LIMIT),
        name="sb_attention",
    )(q, k, v, gn, cum)


def _kv_kernel(mem_ref, wk_ref, wv_ref, k_ref, v_ref):
    nb, ml, d = mem_ref.shape
    m = mem_ref[...].reshape(nb * ml, d).astype(BF16)
    k_ref[...] = _dot(m, wk_ref[...]).reshape(nb, ml, d).astype(k_ref.dtype)
    v_ref[...] = _dot(m, wv_ref[...]).reshape(nb, ml, d).astype(v_ref.dtype)


def _kv_proj(mem, wk_b, wv_b, nb):
    bn, ml, d = mem.shape
    blk = pl.BlockSpec((nb, ml, d), lambda i: (i, 0, 0))
    full = pl.BlockSpec((d, d), lambda i: (0, 0))
    return pl.pallas_call(
        _kv_kernel,
        grid=(bn // nb,),
        in_specs=[blk, full, full],
        out_specs=[blk, blk],
        out_shape=[jax.ShapeDtypeStruct((bn, ml, d), BF16)] * 2,
        compiler_params=pltpu.CompilerParams(
            dimension_semantics=("arbitrary",), vmem_limit_bytes=VMEM_LIMIT),
        name="kv_proj",
    )(mem, wk_b, wv_b)


ROUTE_ROWS = 8
ROUTER_ROWS = 40


def _to_token_tiles(ref, val, ts):
    for c in range(val.shape[1] // LANES):
        ref[pl.ds(c, ts, stride=SUBLANES), :] = val[:, c * LANES:(c + 1) * LANES]


def _from_token_tiles(ref, ts, d):
    return jnp.concatenate(
        [ref[pl.ds(c, ts, stride=SUBLANES), :] for c in range(d // LANES)], axis=1)


def _post_kernel(x_ref, yc_ref, ysb_ref, kb_ref, vb_ref, wo_ref, wq_ref, wmo_ref,
                 ln_ref, wr_ref, br_ref, tri_ref, lt_ref,
                 x2_ref, route_ref, cnt_ref, *, ts):
    d = x_ref.shape[2]
    cw = yc_ref.shape[2]
    x = x_ref[0]
    h1 = _dot(yc_ref[0], wo_ref[0:cw, :]) + _dot(ysb_ref[0], wo_ref[cw:, :])
    x1 = _layer_norm(ALPHA * x + h1, ln_ref[0:1, :], ln_ref[1:2, :])

    hd = d // MEM_HEADS
    q = (_dot(x1.astype(BF16), wq_ref[...]) * (hd ** -0.5)).astype(BF16)
    outs = []
    for h in range(MEM_HEADS):
        sc = _dot_nt(q[:, h * hd:(h + 1) * hd], kb_ref[0, :, h * hd:(h + 1) * hd])
        p = jnp.exp(sc - jnp.max(sc, axis=-1, keepdims=True))
        den = jnp.sum(p, axis=-1, keepdims=True)
        o = _dot(p.astype(BF16), vb_ref[0, :, h * hd:(h + 1) * hd])
        outs.append((o * (1.0 / den)).astype(BF16))
    o = jnp.concatenate(outs, axis=1)
    h2 = _dot(o, wmo_ref[...])
    x2 = _layer_norm(ALPHA * x1 + h2, ln_ref[2:3, :], ln_ref[3:4, :])
    _to_token_tiles(x2_ref, x2, ts)

    lt = _dot_nt(wr_ref[...], x2.astype(BF16)) + br_ref[:, 0:1]
    gl = [lt[g:g + 1, :] for g in range(N_GROUPS)]
    gmax = functools.reduce(jnp.maximum, gl)
    gden = functools.reduce(lambda a, b: a + b, [jnp.exp(l - gmax) for l in gl])
    gprob = 1.0 / gden
    grp = jnp.full(gmax.shape, N_GROUPS - 1, jnp.int32)
    for g in reversed(range(N_GROUPS - 1)):
        grp = jnp.where(gl[g] >= gmax, g, grp)
    el = lt[N_GROUPS:N_GROUPS + EXPERTS_PER_GROUP, :]
    for g in range(1, N_GROUPS):
        lo = N_GROUPS + g * EXPERTS_PER_GROUP
        el = jnp.where(grp == g, lt[lo:lo + EXPERTS_PER_GROUP, :], el)
    row = lax.broadcasted_iota(jnp.int32, el.shape, 0)
    t1 = jnp.max(el, axis=0, keepdims=True)
    i1 = jnp.min(jnp.where(el >= t1, row, EXPERTS_PER_GROUP), axis=0, keepdims=True)
    el2 = jnp.where(row == i1, -jnp.inf, el)
    t2 = jnp.max(el2, axis=0, keepdims=True)
    i2 = jnp.min(jnp.where(el2 >= t2, row, EXPERTS_PER_GROUP), axis=0, keepdims=True)
    e2w = jnp.exp(t2 - t1)
    g0 = gprob / (1.0 + e2w)
    g1 = gprob * e2w / (1.0 + e2w)
    e0 = grp * EXPERTS_PER_GROUP + i1
    e1 = grp * EXPERTS_PER_GROUP + i2

    erow = lax.broadcasted_iota(jnp.int32, (N_EXPERTS, ts), 0)
    oh0 = erow == e0
    oh1 = erow == e1
    both = jnp.where(jnp.logical_or(oh0, oh1), 1.0, 0.0).astype(BF16)
    tot = _dot(both, tri_ref[...])
    lower = _dot(lt_ref[...], both).astype(BF16)
    offs = _dot(lower, tri_ref[:, ts:])
    pos = tot[:, :ts] + offs[:, 0:1]
    p0 = jnp.sum(jnp.where(oh0, pos, 0.0), axis=0, keepdims=True)
    p1 = jnp.sum(jnp.where(oh1, pos, 0.0), axis=0, keepdims=True)
    cnt_ref[0] = tot[:, ts:]

    zero = jnp.zeros_like(g0)
    route_ref[0] = jnp.concatenate(
        [e0.astype(F32), e1.astype(F32), p0, p1, g0, g1, zero, zero], axis=0)


def _post_mixer(x, yc, ysb, kb, vb, wo_b, wq_b, wmo_b, ln, wr_t, br, ts):
    bn, sl, d = x.shape
    cw = yc.shape[2]
    ml = kb.shape[1]
    nt = sl // ts
    j = np.arange(ts)
    tri = jnp.asarray(np.concatenate([j[:, None] < j[None, :], np.ones((ts, LANES), bool)], axis=1), BF16)
    je = np.arange(N_EXPERTS)
    lt = jnp.asarray(je[None, :] < je[:, None], BF16)
    tile = lambda w: pl.BlockSpec((1, ts, w), lambda b, s: (b, s, 0))
    full = lambda a: pl.BlockSpec(a.shape, lambda b, s: (0,) * a.ndim)
    kvspec = pl.BlockSpec((1, ml, d), lambda b, s: (b, 0, 0))
    return pl.pallas_call(
        functools.partial(_post_kernel, ts=ts),
        grid=(bn, nt),
        in_specs=[tile(d), tile(cw), tile(ysb.shape[2]), kvspec, kvspec,
                  full(wo_b), full(wq_b), full(wmo_b), full(ln), full(wr_t), full(br), full(tri),
                  full(lt)],
        out_specs=[
            pl.BlockSpec((ts * SUBLANES, LANES), lambda b, s: (b * nt + s, 0)),
            pl.BlockSpec((1, ROUTE_ROWS, ts), lambda b, s: (b * nt + s, 0, 0)),
            pl.BlockSpec((1, N_EXPERTS, LANES), lambda b, s: (b * nt + s, 0, 0)),
        ],
        out_shape=[
            jax.ShapeDtypeStruct((bn * sl * SUBLANES, LANES), F32),
            jax.ShapeDtypeStruct((bn * nt, ROUTE_ROWS, ts), F32),
            jax.ShapeDtypeStruct((bn * nt, N_EXPERTS, LANES), F32),
        ],
        compiler_params=pltpu.CompilerParams(
            dimension_semantics=("arbitrary", "arbitrary"), vmem_limit_bytes=VMEM_LIMIT),
        name="post_mixer",
    )(x, yc, ysb, kb, vb, wo_b, wq_b, wmo_b, ln, wr_t, br, tri, lt)


def _router_params(ln1_g, ln1_b, ln2_g, ln2_b, w_rg, b_rg, w_re, b_re):
    d = w_rg.shape[0]
    ln = jnp.stack([ln1_g, ln1_b, ln2_g, ln2_b]).astype(F32)
    pad = ROUTER_ROWS - N_GROUPS - N_EXPERTS
    wr_t = jnp.concatenate([w_rg.T, w_re.T, jnp.zeros((pad, d), w_rg.dtype)], axis=0).astype(BF16)
    br = jnp.concatenate([b_rg, b_re, jnp.zeros((pad,), b_rg.dtype)]).astype(F32)
    br = jnp.broadcast_to(br[:, None], (ROUTER_ROWS, LANES))
    return ln, wr_t, br


def _start_copies(cnt_ref, win, list_ref, hbm_ref, buf_ref, sem_of, to_hbm, nbits):
    for b in range(nbits):
        n = (1 << b) * SUBLANES

        def one(i, c, b=b, n=n):
            loc_tok = list_ref[0, 0, b * 2 * N_EXPERTS + i]
            hbm_tok = list_ref[0, 0, (b * 2 + 1) * N_EXPERTS + i]
            loc = buf_ref.at[pl.ds(pl.multiple_of(loc_tok * SUBLANES, SUBLANES), n), :]
            glob = hbm_ref.at[pl.ds(pl.multiple_of(hbm_tok * SUBLANES, SUBLANES), n), :]
            if to_hbm:
                pltpu.make_async_copy(loc, glob, sem_of(b)).start()
            else:
                pltpu.make_async_copy(glob, loc, sem_of(b)).start()
            return c

        lax.fori_loop(0, cnt_ref[win, b], one, 0)


def _wait_copies(cnt_ref, win, hbm_ref, buf_ref, sem_of, to_hbm, nbits):
    for b in range(nbits):
        n = (1 << b) * SUBLANES

        def one(i, c, b=b, n=n):
            loc = buf_ref.at[0:n, :]
            glob = hbm_ref.at[0:n, :]
            if to_hbm:
                pltpu.make_async_copy(loc, glob, sem_of(b)).wait()
            else:
                pltpu.make_async_copy(glob, loc, sem_of(b)).wait()
            return c

        lax.fori_loop(0, cnt_ref[win, b], one, 0)


TOKEN_UNROLL = 8


def _dispatch_kernel(cnt_ref, nu_ref, list_ref, pad_list_ref, pos_ref, x2_ref, xs_ref, sbuf_ref, sems,
                     *, td, tm, nbits):
    w = pl.program_id(0)
    last = pl.num_programs(0) - 1
    slot = w % 2
    scur_ref = sbuf_ref.at[slot]

    @pl.when(w == 0)
    def _():
        block = tm * SUBLANES
        zeros_ref = sbuf_ref.at[1]
        zeros_ref[0:block, :] = jnp.zeros((block, LANES), F32)
        pad_row = pl.num_programs(0)
        sem_of = lambda b: sems.at[1, b]
        _start_copies(cnt_ref, pad_row, pad_list_ref, xs_ref, zeros_ref, sem_of, True, nbits)
        _wait_copies(cnt_ref, pad_row, xs_ref, zeros_ref, sem_of, True, nbits)
        tail = lambda i: pltpu.make_async_copy(
            zeros_ref.at[0:block, :], xs_ref.at[pl.ds(pl.multiple_of(i * block, block), block), :],
            sems.at[1, 0])
        n_blocks = xs_ref.shape[0] // block
        lax.fori_loop(nu_ref[0], n_blocks, lambda i, c: (tail(i).start(), c)[1], 0)
        lax.fori_loop(nu_ref[0], n_blocks, lambda i, c: (tail(i).wait(), c)[1], 0)

    def place(i, c):
        for u in range(TOKEN_UNROLL):
            t = i * TOKEN_UNROLL + u
            tok = x2_ref[pl.ds(pl.multiple_of(t * SUBLANES, SUBLANES), SUBLANES), :]
            for k in range(TOP_K):
                p = pos_ref[0, 0, k * td + t]
                scur_ref[pl.ds(pl.multiple_of(p * SUBLANES, SUBLANES), SUBLANES), :] = tok
        return c

    lax.fori_loop(0, td // TOKEN_UNROLL, place, 0)
    _start_copies(cnt_ref, w, list_ref, xs_ref, scur_ref, lambda b: sems.at[slot, b], True, nbits)

    @pl.when(w > 0)
    def _():
        _wait_copies(cnt_ref, w - 1, xs_ref, sbuf_ref.at[1 - slot], lambda b: sems.at[1 - slot, b],
                     True, nbits)

    @pl.when(w == last)
    def _():
        _wait_copies(cnt_ref, w, xs_ref, scur_ref, lambda b: sems.at[slot, b], True, nbits)


def _dispatch(counts, n_used, lists, pos, x2t, n_rows, td, tm):
    nt = pos.shape[0]
    nbits = counts.shape[1]
    assert tm <= TOP_K * td and n_rows % tm == 0 and nbits == td.bit_length()
    list_len = lists.shape[2]
    return pl.pallas_call(
        functools.partial(_dispatch_kernel, td=td, tm=tm, nbits=nbits),
        grid_spec=pltpu.PrefetchScalarGridSpec(
            num_scalar_prefetch=2,
            grid=(nt,),
            in_specs=[
                pl.BlockSpec((1, 1, list_len), lambda i, c, nu: (i, 0, 0), memory_space=pltpu.SMEM),
                pl.BlockSpec((1, 1, list_len), lambda i, c, nu: (nt, 0, 0), memory_space=pltpu.SMEM),
                pl.BlockSpec((1, 1, TOP_K * td), lambda i, c, nu: (i, 0, 0), memory_space=pltpu.SMEM),
                pl.BlockSpec((td * SUBLANES, LANES), lambda i, c, nu: (i, 0)),
            ],
            out_specs=pl.BlockSpec(memory_space=pl.ANY),
            scratch_shapes=[pltpu.VMEM((2, TOP_K * td * SUBLANES, LANES), F32),
                            pltpu.SemaphoreType.DMA((2, nbits))],
        ),
        out_shape=jax.ShapeDtypeStruct((n_rows * SUBLANES, LANES), F32),
        compiler_params=pltpu.CompilerParams(
            dimension_semantics=("arbitrary",), vmem_limit_bytes=VMEM_LIMIT),
        name="dispatch",
    )(counts, n_used, lists, lists, pos, x2t)


def _expert_kernel(be_ref, nu_ref, xs_ref, wg_ref, wu_ref, wd_ref, ys_ref, *, tm):
    del be_ref
    d = wg_ref.shape[1]

    @pl.when(pl.program_id(0) < nu_ref[0])
    def _():
        xb = _from_token_tiles(xs_ref, tm, d).astype(BF16)
        g = _dot(xb, wg_ref[0])
        u = _dot(xb, wu_ref[0])
        h = g * jax.nn.sigmoid(g) * u
        _to_token_tiles(ys_ref, _dot(h.astype(BF16), wd_ref[0]), tm)

    @pl.when(pl.program_id(0) >= nu_ref[0])
    def _():
        ys_ref[...] = jnp.zeros(ys_ref.shape, ys_ref.dtype)


def _experts(block_expert, n_used, xs, wg_b, wu_b, wd_b, tm):
    ne, d, f = wg_b.shape
    nb = xs.shape[0] // (tm * SUBLANES)
    rows = lambda i, be, nu: (i, 0)
    wsel = lambda i, be, nu: (be[i], 0, 0)
    return pl.pallas_call(
        functools.partial(_expert_kernel, tm=tm),
        grid_spec=pltpu.PrefetchScalarGridSpec(
            num_scalar_prefetch=2,
            grid=(nb,),
            in_specs=[
                pl.BlockSpec((tm * SUBLANES, LANES), rows),
                pl.BlockSpec((1, d, f), wsel),
                pl.BlockSpec((1, d, f), wsel),
                pl.BlockSpec((1, f, d), wsel),
            ],
            out_specs=pl.BlockSpec((tm * SUBLANES, LANES), rows),
        ),
        out_shape=jax.ShapeDtypeStruct(xs.shape, F32),
        compiler_params=pltpu.CompilerParams(
            dimension_semantics=("arbitrary",), vmem_limit_bytes=VMEM_LIMIT),
        name="experts",
    )(block_expert, n_used, xs, wg_b, wu_b, wd_b)


def _combine_kernel(cnt_ref, list_ref, next_list_ref, pos_ref, gate_ref, x2_ref, ln_ref, ys_ref, o_ref,
                    ybuf_ref, rbuf_ref, sems, *, tc, nbits):
    w = pl.program_id(0)
    d = o_ref.shape[1]
    slot = w % 2
    ycur_ref = ybuf_ref.at[slot]

    @pl.when(w == 0)
    def _():
        _start_copies(cnt_ref, w, list_ref, ys_ref, ycur_ref, lambda b: sems.at[slot, b], False, nbits)

    @pl.when(w + 1 < pl.num_programs(0))
    def _():
        _start_copies(cnt_ref, w + 1, next_list_ref, ys_ref, ybuf_ref.at[1 - slot],
                      lambda b: sems.at[1 - slot, b], False, nbits)

    _wait_copies(cnt_ref, w, ys_ref, ycur_ref, lambda b: sems.at[slot, b], False, nbits)

    def gather(i, c):
        for u in range(TOKEN_UNROLL):
            t = i * TOKEN_UNROLL + u
            rows = pl.ds(pl.multiple_of(t * SUBLANES, SUBLANES), SUBLANES)
            moe = None
            for k in range(TOP_K):
                p = pos_ref[0, 0, k * tc + t]
                y = ycur_ref[pl.ds(pl.multiple_of(p * SUBLANES, SUBLANES), SUBLANES), :]
                term = gate_ref[0, 0, k * tc + t] * y
                moe = term if moe is None else moe + term
            rbuf_ref[rows, :] = ALPHA * x2_ref[rows, :] + moe
        return c

    lax.fori_loop(0, tc // TOKEN_UNROLL, gather, 0)
    o_ref[...] = _layer_norm(_from_token_tiles(rbuf_ref, tc, d), ln_ref[0:1, :], ln_ref[1:2, :])


def _combine(counts, lists, pos, gates, x2t, ln3, ys, tc):
    nt = pos.shape[0]
    d = ln3.shape[1]
    nbits = counts.shape[1]
    list_len = lists.shape[2]
    smem_tile = pl.BlockSpec((1, 1, TOP_K * tc), lambda i, c: (i, 0, 0), memory_space=pltpu.SMEM)
    return pl.pallas_call(
        functools.partial(_combine_kernel, tc=tc, nbits=nbits),
        grid_spec=pltpu.PrefetchScalarGridSpec(
            num_scalar_prefetch=1,
            grid=(nt,),
            in_specs=[
                pl.BlockSpec((1, 1, list_len), lambda i, c: (i, 0, 0), memory_space=pltpu.SMEM),
                pl.BlockSpec((1, 1, list_len), lambda i, c: (jnp.minimum(i + 1, nt - 1), 0, 0),
                             memory_space=pltpu.SMEM),
                smem_tile,
                smem_tile,
                pl.BlockSpec((tc * SUBLANES, LANES), lambda i, c: (i, 0)),
                pl.BlockSpec(ln3.shape, lambda i, c: (0, 0)),
                pl.BlockSpec(memory_space=pl.ANY),
            ],
            out_specs=pl.BlockSpec((tc, d), lambda i, c: (i, 0)),
            scratch_shapes=[pltpu.VMEM((2, TOP_K * tc * SUBLANES, LANES), F32),
                            pltpu.VMEM((tc * SUBLANES, LANES), F32),
                            pltpu.SemaphoreType.DMA((2, nbits))],
        ),
        out_shape=jax.ShapeDtypeStruct((nt * tc, d), F32),
        compiler_params=pltpu.CompilerParams(
            dimension_semantics=("arbitrary",), vmem_limit_bytes=VMEM_LIMIT),
        name="combine",
    )(counts, lists, lists, pos, gates, x2t, ln3, ys)


def _routing_plan(route, wcnt, tm):
    nt, _, ts = route.shape
    nbits = ts.bit_length()
    cw = wcnt[:, :, 0].astype(jnp.int32)
    cnt = jnp.sum(cw, axis=0)
    padded = (cnt + tm - 1) // tm * tm
    pend = jnp.cumsum(padded)
    pstart = pend - padded
    base = jnp.cumsum(cw, axis=0) - cw
    offs = jnp.cumsum(cw, axis=1) - cw
    c_all = jnp.concatenate([cw, (padded - cnt)[None, :]], axis=0)
    o_all = jnp.concatenate([offs, jnp.zeros_like(cnt)[None, :]], axis=0)
    d_all = jnp.concatenate([pstart[None, :] + base, (pstart + cnt)[None, :]], axis=0)
    j = jnp.arange(N_EXPERTS, dtype=jnp.int32)
    per_class, counts = [], []
    for b in range(nbits):
        has = (c_all >> b) & 1
        done = (c_all >> (b + 1)) << (b + 1)
        slot = jnp.cumsum(has, axis=1) - has
        pick = ((slot[:, :, None] == j[None, None, :]) & (has[:, :, None] == 1)).astype(jnp.int32)
        per_class.append(jnp.einsum('wej,we->wj', pick, o_all + done))
        per_class.append(jnp.einsum('wej,we->wj', pick, d_all + done))
        counts.append(jnp.sum(has, axis=1))
    lists = jnp.concatenate(per_class, axis=1).reshape(nt + 1, 1, nbits * 2 * N_EXPERTS)
    counts = jnp.stack(counts, axis=1)
    pos = route[:, 2:4, :].astype(jnp.int32).reshape(nt, 1, TOP_K * ts)
    gates = route[:, 4:6, :].reshape(nt, 1, TOP_K * ts)
    nb = nt * ts * TOP_K // tm + N_EXPERTS
    first_row = jnp.arange(nb, dtype=jnp.int32) * tm
    block_expert = jnp.minimum(
        jnp.sum((pend[None, :] <= first_row[:, None]).astype(jnp.int32), axis=1), N_EXPERTS - 1)
    n_used = (pend[-1:] // tm).astype(jnp.int32)
    return lists, counts, pos, gates, block_expert, n_used, nb


def _moe(x2t, route, wcnt, wg_b, wu_b, wd_b, ln3, tm):
    ts = route.shape[2]
    lists, counts, pos, gates, block_expert, n_used, nb = _routing_plan(route, wcnt, tm)
    xs = _dispatch(counts, n_used, lists, pos, x2t, nb * tm, ts, tm)
    ys = _experts(block_expert, n_used, xs, wg_b, wu_b, wd_b, tm)
    return _combine(counts, lists, pos, gates, x2t, ln3, ys, ts)


def kernel(x, mem, w_in, conv_w, gn_conv, gn_sb, w_out, ln1_g, ln1_b, w_mq, w_mk, w_mv, w_mo,
           ln2_g, ln2_b, w_router_group, b_router_group, w_router_expert, b_router_expert,
           w_gate, w_up, w_down, ln3_g, ln3_b):
    for l in range(DEPTH):
        x = _layer(x, mem, w_in[l], conv_w[l], gn_conv[l], gn_sb[l], w_out[l], ln1_g[l], ln1_b[l],
                   w_mq[l], w_mk[l], w_mv[l], w_mo[l], ln2_g[l], ln2_b[l],
                   w_router_group[l], b_router_group[l], w_router_expert[l], b_router_expert[l],
                   w_gate[l], w_up[l], w_down[l], ln3_g[l], ln3_b[l])
    return x


def _layer(x, mem, w_in, conv_w, gn_conv, gn_sb, w_out, ln1_g, ln1_b, w_mq, w_mk, w_mv, w_mo,
           ln2_g, ln2_b, w_rg, b_rg, w_re, b_re, w_gate, w_up, w_down, ln3_g, ln3_b,
           ts=512, tq=512, tm=512, kv_nb=4):
    bn, sl, d = x.shape
    bf = lambda w: w.astype(BF16)
    yc, q, k, v = _proj_conv(x, bf(w_in), conv_w, gn_conv, ts)
    ysb = _sb_attention(q, k, v, gn_sb, tq)
    kb, vb = _kv_proj(mem, bf(w_mk), bf(w_mv), kv_nb)
    ln, wr_t, br = _router_params(ln1_g, ln1_b, ln2_g, ln2_b, w_rg, b_rg, w_re, b_re)
    x2t, route, wcnt = _post_mixer(x, yc, ysb, kb, vb, bf(w_out), bf(w_mq), bf(w_mo),
                                   ln, wr_t, br, ts)
    ln3 = jnp.stack([ln3_g, ln3_b]).astype(F32)
    out = _moe(x2t, route, wcnt, bf(w_gate), bf(w_up), bf(w_down), ln3, tm)
    return out.reshape(bn, sl, d)
```

```python
import functools

import jax
import jax.numpy as jnp
import numpy as np
from jax import lax
from jax.experimental import pallas as pl
from jax.experimental.pallas import tpu as pltpu

CONV_WIDTH = 512
CONV_GROUP = 64
CONV_K = 3
SB_HEADS = 8
SB_HEAD_DIM = 64
SB_WIDTH = SB_HEADS * SB_HEAD_DIM
MEM_HEADS = 4
N_GROUPS = 4
EXPERTS_PER_GROUP = 8
N_EXPERTS = N_GROUPS * EXPERTS_PER_GROUP
TOP_K = 2
DEPTH = 1
ALPHA = (2 * DEPTH) ** 0.25
LN_EPS = 1e-5
LOG2E = 1.4426950408889634
RMS_EPS = 1e-6

LANES = 128
SUBLANES = 8
VMEM_LIMIT = 56 * 1024 * 1024

BF16 = jnp.bfloat16
F32 = jnp.float32


def _split_bf16(a):
    hi = a.astype(BF16)
    lo = (a - hi.astype(F32)).astype(BF16)
    return hi, lo


def _dot(a, b):
    return jnp.dot(a, b, preferred_element_type=F32)


def _dot_nt(a, b):
    return lax.dot_general(a, b, (((1,), (1,)), ((), ())), preferred_element_type=F32)


def _layer_norm(r, g, b):
    mu = jnp.mean(r, axis=-1, keepdims=True)
    c = r - mu
    var = jnp.mean(c * c, axis=-1, keepdims=True)
    return c * lax.rsqrt(var + LN_EPS) * g + b


def _proj_conv_kernel(x_ref, w_ref, cw_ref, gn_ref, gmat_ref,
                      yc_ref, q_ref, k_ref, v_ref, ubuf_ref, *, ts):
    s = pl.program_id(1)
    xb = x_ref[0].astype(BF16)
    cw = CONV_WIDTH

    @pl.when(s == 0)
    def _():
        ubuf_ref[0:SUBLANES, :] = jnp.zeros((SUBLANES, cw), F32)

    cc = _dot(xb, w_ref[:, cw:2 * cw])
    ch = _dot(xb, w_ref[:, 2 * cw:3 * cw])
    u = cc * ch
    ubuf_ref[SUBLANES:SUBLANES + ts, :] = u
    um1 = ubuf_ref[SUBLANES - 1:SUBLANES - 1 + ts, :]
    um2 = ubuf_ref[SUBLANES - 2:SUBLANES - 2 + ts, :]
    conv = cw_ref[2:3, :] * u + cw_ref[1:2, :] * um1 + cw_ref[0:1, :] * um2
    ubuf_ref[0:SUBLANES, :] = ubuf_ref[ts:ts + SUBLANES, :]

    cb = _dot(xb, w_ref[:, 0:cw])
    y = cb * conv
    hi, lo = _split_bf16(y * y)
    gsum = _dot(hi, gmat_ref[...]) + _dot(lo, gmat_ref[...])
    yn = y * lax.rsqrt(gsum * (1.0 / CONV_GROUP) + RMS_EPS) * gn_ref[...]
    yc_ref[0] = yn.astype(yc_ref.dtype)

    sw = SB_WIDTH
    base = 3 * cw
    q_ref[0] = (_dot(xb, w_ref[:, base:base + sw]) * (SB_HEAD_DIM ** -0.5 * LOG2E)).astype(q_ref.dtype)
    k_ref[0] = _dot(xb, w_ref[:, base + sw:base + 2 * sw]).astype(k_ref.dtype)
    v_ref[0] = _dot(xb, w_ref[:, base + 2 * sw:base + 3 * sw]).astype(v_ref.dtype)


def _proj_conv(x, w_in_b, conv_w, gn_conv, ts):
    bn, sl, d = x.shape
    gidx = np.arange(CONV_WIDTH) // CONV_GROUP
    gmat = jnp.asarray(gidx[:, None] == gidx[None, :], BF16)
    outs = [jax.ShapeDtypeStruct((bn, sl, CONV_WIDTH), BF16)] * 4
    tile = lambda w: pl.BlockSpec((1, ts, w), lambda b, s: (b, s, 0))
    full = lambda a: pl.BlockSpec(a.shape, lambda b, s: (0,) * a.ndim)
    gn = gn_conv.reshape(1, CONV_WIDTH)
    return pl.pallas_call(
        functools.partial(_proj_conv_kernel, ts=ts),
        grid=(bn, sl // ts),
        in_specs=[tile(d), full(w_in_b), full(conv_w), full(gn), full(gmat)],
        out_specs=[tile(CONV_WIDTH)] * 4,
        out_shape=outs,
        scratch_shapes=[pltpu.VMEM((ts + 2 * SUBLANES, CONV_WIDTH), F32)],
        compiler_params=pltpu.CompilerParams(
            dimension_semantics=("arbitrary", "arbitrary"), vmem_limit_bytes=VMEM_LIMIT),
        name="proj_conv",
    )(x, w_in_b, conv_w, gn, gmat)


SB_TK = 128
SB_UNROLL = 2
SB_ROW_GROUPS = 2
SB_DEAD = 192.0
BF16_BITS_OF_F32 = 0xFFFF0000


def _bits(a):
    return lax.bitcast_convert_type(a, jnp.uint32)


def _softplus2(z2):
    pos = jnp.maximum(z2, 0.0)
    return pos + jnp.log2(1.0 + jnp.exp2(z2 - 2.0 * pos))


def _sb_kernel(q_ref, k_ref, v_ref, gn_ref, cum_ref, o_ref, acc_ref, run_ref, *, tq):
    qi = pl.program_id(2)
    tk = SB_TK
    nd = tq // tk
    lane = lax.broadcasted_iota(jnp.int32, (1, LANES), 1)
    head0 = lane < SB_HEAD_DIM
    zero = jnp.zeros((), BF16)

    def per_head_rows(ref, kstart):
        blk = ref[0, pl.ds(kstart, tk), :]
        return jnp.concatenate([jnp.where(head0, blk, zero), jnp.where(head0, zero, blk)], axis=0)

    def tile(row0, row1, kstart, diag, runs):
        rows = row1 - row0
        z2 = _dot_nt(q_ref[0, row0:row1, :], per_head_rows(k_ref, kstart))
        if diag:
            t_loc = lax.broadcasted_iota(jnp.int32, (tk, tk), 0)
            s_loc = lax.broadcasted_iota(jnp.int32, (tk, tk), 1)
            before = s_loc < t_loc
            causal = lambda m: jnp.concatenate(
                [jnp.where(before, m[:tk], 0.0)] + ([m[tk:]] if rows > tk else []), axis=0)
        a_parts, new_runs = [], []
        for h in range(2):
            zh = z2[:, h * tk:(h + 1) * tk]
            sp = _softplus2(zh)
            if diag:
                sp = causal(sp)
            hi = lax.bitcast_convert_type(_bits(sp) & jnp.uint32(BF16_BITS_OF_F32), F32)
            lhs = jnp.concatenate([hi.astype(BF16), (sp - hi).astype(BF16)], axis=1)
            cs = _dot(lhs, cum_ref[...])
            a = jnp.exp2(zh - cs[:, :tk] - runs[h])
            if diag:
                a = causal(a)
            a_parts.append(a.astype(BF16))
            new_runs.append(runs[h] + cs[:, tk:])
        return jnp.concatenate(a_parts, axis=1), new_runs

    acc = None
    runs = None
    for d in reversed(range(nd)):
        row0 = d * tk
        kstart = pl.multiple_of(qi * tq + row0, tk)
        fresh = jnp.zeros((tk, LANES), F32)
        runs = [fresh, fresh] if runs is None else [jnp.concatenate([fresh, r], axis=0) for r in runs]
        a, runs = tile(row0, tq, kstart, True, runs)
        av = _dot(a, per_head_rows(v_ref, kstart))
        acc = av if acc is None else av + jnp.concatenate([fresh, acc], axis=0)
    acc_ref[...] = acc
    run_ref[0] = runs[0]
    run_ref[1] = runs[1]

    n_iter = qi * nd // SB_UNROLL

    def all_dead(runs):
        return (jnp.min(jnp.minimum(runs[0], runs[1])) > SB_DEAD).astype(jnp.int32)

    group = tq // SB_ROW_GROUPS
    for g in range(SB_ROW_GROUPS):
        r0, r1 = g * group, (g + 1) * group

        def more(carry):
            j, dead = carry
            return jnp.logical_and(j < n_iter, dead == 0)

        def body(carry, r0=r0, r1=r1):
            j, _ = carry
            runs = [run_ref[0, r0:r1, :], run_ref[1, r0:r1, :]]
            a_parts, v_parts = [], []
            for u in range(SB_UNROLL):
                kstart = pl.multiple_of((qi * nd - 1 - (j * SB_UNROLL + u)) * tk, tk)
                a, runs = tile(r0, r1, kstart, False, runs)
                a_parts.append(a)
                v_parts.append(per_head_rows(v_ref, kstart))
            acc_ref[r0:r1, :] += _dot(jnp.concatenate(a_parts, axis=1), jnp.concatenate(v_parts, axis=0))
            run_ref[0, r0:r1, :] = runs[0]
            run_ref[1, r0:r1, :] = runs[1]
            return j + 1, all_dead(runs)

        start = all_dead([run_ref[0, r0:r1, :], run_ref[1, r0:r1, :]])
        lax.while_loop(more, body, (jnp.int32(0), start))

    y = acc_ref[...]
    y2 = y * y
    s0 = jnp.sum(jnp.where(head0, y2, 0.0), axis=-1, keepdims=True)
    s1 = jnp.sum(jnp.where(head0, 0.0, y2), axis=-1, keepdims=True)
    ms = jnp.where(head0, s0, s1) * (1.0 / SB_HEAD_DIM)
    o_ref[0] = (y * lax.rsqrt(ms + RMS_EPS) * gn_ref[...]).astype(o_ref.dtype)


def _sb_attention(q, k, v, gn_sb, tq):
    bn, sl, w = q.shape
    npair = w // LANES
    j = np.arange(SB_TK)
    assert (tq // SB_TK) % SB_UNROLL == 0
    cum = np.concatenate([(j[:, None] >= j[None, :]), np.ones((SB_TK, SB_TK), bool)], axis=1)
    cum = jnp.asarray(np.concatenate([cum, cum], axis=0), BF16)
    gn = gn_sb.reshape(1, w)
    return pl.pallas_call(
        functools.partial(_sb_kernel, tq=tq),
        grid=(bn, npair, sl // tq),
        in_specs=[
            pl.BlockSpec((1, tq, LANES), lambda b, p, i: (b, i, p)),
            pl.BlockSpec((1, sl, LANES), lambda b, p, i: (b, 0, p)),
            pl.BlockSpec((1, sl, LANES), lambda b, p, i: (b, 0, p)),
            pl.BlockSpec((1, LANES), lambda b, p, i: (0, p)),
            pl.BlockSpec(cum.shape, lambda b, p, i: (0, 0)),
        ],
        out_specs=pl.BlockSpec((1, tq, LANES), lambda b, p, i: (b, i, p)),
        out_shape=jax.ShapeDtypeStruct((bn, sl, w), BF16),
        scratch_shapes=[pltpu.VMEM((tq, LANES), F32), pltpu.VMEM((2, tq, LANES), F32)],
        compiler_params=pltpu.CompilerParams(
            dimension_semantics=("arbitrary", "arbitrary", "arbitrary"),
            vmem_limit_bytes=VMEM_LIMIT),
        name="sb_attention",
    )(q, k, v, gn, cum)


def _kv_kernel(mem_ref, wk_ref, wv_ref, k_ref, v_ref):
    nb, ml, d = mem_ref.shape
    m = mem_ref[...].reshape(nb * ml, d).astype(BF16)
    k_ref[...] = _dot(m, wk_ref[...]).reshape(nb, ml, d).astype(k_ref.dtype)
    v_ref[...] = _dot(m, wv_ref[...]).reshape(nb, ml, d).astype(v_ref.dtype)


def _kv_proj(mem, wk_b, wv_b, nb):
    bn, ml, d = mem.shape
    blk = pl.BlockSpec((nb, ml, d), lambda i: (i, 0, 0))
    full = pl.BlockSpec((d, d), lambda i: (0, 0))
    return pl.pallas_call(
        _kv_kernel,
        grid=(bn // nb,),
        in_specs=[blk, full, full],
        out_specs=[blk, blk],
        out_shape=[jax.ShapeDtypeStruct((bn, ml, d), BF16)] * 2,
        compiler_params=pltpu.CompilerParams(
            dimension_semantics=("arbitrary",), vmem_limit_bytes=VMEM_LIMIT),
        name="kv_proj",
    )(mem, wk_b, wv_b)


ROUTE_ROWS = 8
ROUTER_ROWS = 40


def _to_token_tiles(ref, val, ts):
    for c in range(val.shape[1] // LANES):
        ref[pl.ds(c, ts, stride=SUBLANES), :] = val[:, c * LANES:(c + 1) * LANES]


def _from_token_tiles(ref, ts, d):
    return jnp.concatenate(
        [ref[pl.ds(c, ts, stride=SUBLANES), :] for c in range(d // LANES)], axis=1)


def _post_kernel(x_ref, yc_ref, ysb_ref, kb_ref, vb_ref, wo_ref, wq_ref, wmo_ref,
                 ln_ref, wr_ref, br_ref, tri_ref, lt_ref,
                 x2_ref, route_ref, cnt_ref, *, ts):
    d = x_ref.shape[2]
    cw = yc_ref.shape[2]
    x = x_ref[0]
    h1 = _dot(yc_ref[0], wo_ref[0:cw, :]) + _dot(ysb_ref[0], wo_ref[cw:, :])
    x1 = _layer_norm(ALPHA * x + h1, ln_ref[0:1, :], ln_ref[1:2, :])

    hd = d // MEM_HEADS
    q = (_dot(x1.astype(BF16), wq_ref[...]) * (hd ** -0.5)).astype(BF16)
    outs = []
    for h in range(MEM_HEADS):
        sc = _dot_nt(q[:, h * hd:(h + 1) * hd], kb_ref[0, :, h * hd:(h + 1) * hd])
        p = jnp.exp(sc - jnp.max(sc, axis=-1, keepdims=True))
        den = jnp.sum(p, axis=-1, keepdims=True)
        o = _dot(p.astype(BF16), vb_ref[0, :, h * hd:(h + 1) * hd])
        outs.append((o * (1.0 / den)).astype(BF16))
    o = jnp.concatenate(outs, axis=1)
    h2 = _dot(o, wmo_ref[...])
    x2 = _layer_norm(ALPHA * x1 + h2, ln_ref[2:3, :], ln_ref[3:4, :])
    _to_token_tiles(x2_ref, x2, ts)

    lt = _dot_nt(wr_ref[...], x2.astype(BF16)) + br_ref[:, 0:1]
    gl = [lt[g:g + 1, :] for g in range(N_GROUPS)]
    gmax = functools.reduce(jnp.maximum, gl)
    gden = functools.reduce(lambda a, b: a + b, [jnp.exp(l - gmax) for l in gl])
    gprob = 1.0 / gden
    grp = jnp.full(gmax.shape, N_GROUPS - 1, jnp.int32)
    for g in reversed(range(N_GROUPS - 1)):
        grp = jnp.where(gl[g] >= gmax, g, grp)
    el = lt[N_GROUPS:N_GROUPS + EXPERTS_PER_GROUP, :]
    for g in range(1, N_GROUPS):
        lo = N_GROUPS + g * EXPERTS_PER_GROUP
        el = jnp.where(grp == g, lt[lo:lo + EXPERTS_PER_GROUP, :], el)
    row = lax.broadcasted_iota(jnp.int32, el.shape, 0)
    t1 = jnp.max(el, axis=0, keepdims=True)
    i1 = jnp.min(jnp.where(el >= t1, row, EXPERTS_PER_GROUP), axis=0, keepdims=True)
    el2 = jnp.where(row == i1, -jnp.inf, el)
    t2 = jnp.max(el2, axis=0, keepdims=True)
    i2 = jnp.min(jnp.where(el2 >= t2, row, EXPERTS_PER_GROUP), axis=0, keepdims=True)
    e2w = jnp.exp(t2 - t1)
    g0 = gprob / (1.0 + e2w)
    g1 = gprob * e2w / (1.0 + e2w)
    e0 = grp * EXPERTS_PER_GROUP + i1
    e1 = grp * EXPERTS_PER_GROUP + i2

    erow = lax.broadcasted_iota(jnp.int32, (N_EXPERTS, ts), 0)
    oh0 = erow == e0
    oh1 = erow == e1
    both = jnp.where(jnp.logical_or(oh0, oh1), 1.0, 0.0).astype(BF16)
    tot = _dot(both, tri_ref[...])
    lower = _dot(lt_ref[...], both).astype(BF16)
    offs = _dot(lower, tri_ref[:, ts:])
    pos = tot[:, :ts] + offs[:, 0:1]
    p0 = jnp.sum(jnp.where(oh0, pos, 0.0), axis=0, keepdims=True)
    p1 = jnp.sum(jnp.where(oh1, pos, 0.0), axis=0, keepdims=True)
    cnt_ref[0] = tot[:, ts:]

    zero = jnp.zeros_like(g0)
    route_ref[0] = jnp.concatenate(
        [e0.astype(F32), e1.astype(F32), p0, p1, g0, g1, zero, zero], axis=0)


def _post_mixer(x, yc, ysb, kb, vb, wo_b, wq_b, wmo_b, ln, wr_t, br, ts):
    bn, sl, d = x.shape
    cw = yc.shape[2]
    ml = kb.shape[1]
    nt = sl // ts
    j = np.arange(ts)
    tri = jnp.asarray(np.concatenate([j[:, None] < j[None, :], np.ones((ts, LANES), bool)], axis=1), BF16)
    je = np.arange(N_EXPERTS)
    lt = jnp.asarray(je[None, :] < je[:, None], BF16)
    tile = lambda w: pl.BlockSpec((1, ts, w), lambda b, s: (b, s, 0))
    full = lambda a: pl.BlockSpec(a.shape, lambda b, s: (0,) * a.ndim)
    kvspec = pl.BlockSpec((1, ml, d), lambda b, s: (b, 0, 0))
    return pl.pallas_call(
        functools.partial(_post_kernel, ts=ts),
        grid=(bn, nt),
        in_specs=[tile(d), tile(cw), tile(ysb.shape[2]), kvspec, kvspec,
                  full(wo_b), full(wq_b), full(wmo_b), full(ln), full(wr_t), full(br), full(tri),
                  full(lt)],
        out_specs=[
            pl.BlockSpec((ts * SUBLANES, LANES), lambda b, s: (b * nt + s, 0)),
            pl.BlockSpec((1, ROUTE_ROWS, ts), lambda b, s: (b * nt + s, 0, 0)),
            pl.BlockSpec((1, N_EXPERTS, LANES), lambda b, s: (b * nt + s, 0, 0)),
        ],
        out_shape=[
            jax.ShapeDtypeStruct((bn * sl * SUBLANES, LANES), F32),
            jax.ShapeDtypeStruct((bn * nt, ROUTE_ROWS, ts), F32),
            jax.ShapeDtypeStruct((bn * nt, N_EXPERTS, LANES), F32),
        ],
        compiler_params=pltpu.CompilerParams(
            dimension_semantics=("arbitrary", "arbitrary"), vmem_limit_bytes=VMEM_LIMIT),
        name="post_mixer",
    )(x, yc, ysb, kb, vb, wo_b, wq_b, wmo_b, ln, wr_t, br, tri, lt)


def _router_params(ln1_g, ln1_b, ln2_g, ln2_b, w_rg, b_rg, w_re, b_re):
    d = w_rg.shape[0]
    ln = jnp.stack([ln1_g, ln1_b, ln2_g, ln2_b]).astype(F32)
    pad = ROUTER_ROWS - N_GROUPS - N_EXPERTS
    wr_t = jnp.concatenate([w_rg.T, w_re.T, jnp.zeros((pad, d), w_rg.dtype)], axis=0).astype(BF16)
    br = jnp.concatenate([b_rg, b_re, jnp.zeros((pad,), b_rg.dtype)]).astype(F32)
    br = jnp.broadcast_to(br[:, None], (ROUTER_ROWS, LANES))
    return ln, wr_t, br


def _start_copies(cnt_ref, win, list_ref, hbm_ref, buf_ref, sem_of, to_hbm, nbits):
    for b in range(nbits):
        n = (1 << b) * SUBLANES

        def one(i, c, b=b, n=n):
            loc_tok = list_ref[0, 0, b * 2 * N_EXPERTS + i]
            hbm_tok = list_ref[0, 0, (b * 2 + 1) * N_EXPERTS + i]
            loc = buf_ref.at[pl.ds(pl.multiple_of(loc_tok * SUBLANES, SUBLANES), n), :]
            glob = hbm_ref.at[pl.ds(pl.multiple_of(hbm_tok * SUBLANES, SUBLANES), n), :]
            if to_hbm:
                pltpu.make_async_copy(loc, glob, sem_of(b)).start()
            else:
                pltpu.make_async_copy(glob, loc, sem_of(b)).start()
            return c

        lax.fori_loop(0, cnt_ref[win, b], one, 0)


def _wait_copies(cnt_ref, win, hbm_ref, buf_ref, sem_of, to_hbm, nbits):
    for b in range(nbits):
        n = (1 << b) * SUBLANES

        def one(i, c, b=b, n=n):
            loc = buf_ref.at[0:n, :]
            glob = hbm_ref.at[0:n, :]
            if to_hbm:
                pltpu.make_async_copy(loc, glob, sem_of(b)).wait()
            else:
                pltpu.make_async_copy(glob, loc, sem_of(b)).wait()
            return c

        lax.fori_loop(0, cnt_ref[win, b], one, 0)


TOKEN_UNROLL = 8


def _dispatch_kernel(cnt_ref, nu_ref, list_ref, pad_list_ref, pos_ref, x2_ref, xs_ref, sbuf_ref, sems,
                     *, td, tm, nbits):
    w = pl.program_id(0)
    last = pl.num_programs(0) - 1
    slot = w % 2
    scur_ref = sbuf_ref.at[slot]

    @pl.when(w == 0)
    def _():
        block = tm * SUBLANES
        zeros_ref = sbuf_ref.at[1]
        zeros_ref[0:block, :] = jnp.zeros((block, LANES), F32)
        pad_row = pl.num_programs(0)
        sem_of = lambda b: sems.at[1, b]
        _start_copies(cnt_ref, pad_row, pad_list_ref, xs_ref, zeros_ref, sem_of, True, nbits)
        _wait_copies(cnt_ref, pad_row, xs_ref, zeros_ref, sem_of, True, nbits)
        tail = lambda i: pltpu.make_async_copy(
            zeros_ref.at[0:block, :], xs_ref.at[pl.ds(pl.multiple_of(i * block, block), block), :],
            sems.at[1, 0])
        n_blocks = xs_ref.shape[0] // block
        lax.fori_loop(nu_ref[0], n_blocks, lambda i, c: (tail(i).start(), c)[1], 0)
        lax.fori_loop(nu_ref[0], n_blocks, lambda i, c: (tail(i).wait(), c)[1], 0)

    def place(i, c):
        for u in range(TOKEN_UNROLL):
            t = i * TOKEN_UNROLL + u
            tok = x2_ref[pl.ds(pl.multiple_of(t * SUBLANES, SUBLANES), SUBLANES), :]
            for k in range(TOP_K):
                p = pos_ref[0, 0, k * td + t]
                scur_ref[pl.ds(pl.multiple_of(p * SUBLANES, SUBLANES), SUBLANES), :] = tok
        return c

    lax.fori_loop(0, td // TOKEN_UNROLL, place, 0)
    _start_copies(cnt_ref, w, list_ref, xs_ref, scur_ref, lambda b: sems.at[slot, b], True, nbits)

    @pl.when(w > 0)
    def _():
        _wait_copies(cnt_ref, w - 1, xs_ref, sbuf_ref.at[1 - slot], lambda b: sems.at[1 - slot, b],
                     True, nbits)

    @pl.when(w == last)
    def _():
        _wait_copies(cnt_ref, w, xs_ref, scur_ref, lambda b: sems.at[slot, b], True, nbits)


def _dispatch(counts, n_used, lists, pos, x2t, n_rows, td, tm):
    nt = pos.shape[0]
    nbits = counts.shape[1]
    assert tm <= TOP_K * td and n_rows % tm == 0 and nbits == td.bit_length()
    list_len = lists.shape[2]
    return pl.pallas_call(
        functools.partial(_dispatch_kernel, td=td, tm=tm, nbits=nbits),
        grid_spec=pltpu.PrefetchScalarGridSpec(
            num_scalar_prefetch=2,
            grid=(nt,),
            in_specs=[
                pl.BlockSpec((1, 1, list_len), lambda i, c, nu: (i, 0, 0), memory_space=pltpu.SMEM),
                pl.BlockSpec((1, 1, list_len), lambda i, c, nu: (nt, 0, 0), memory_space=pltpu.SMEM),
                pl.BlockSpec((1, 1, TOP_K * td), lambda i, c, nu: (i, 0, 0), memory_space=pltpu.SMEM),
                pl.BlockSpec((td * SUBLANES, LANES), lambda i, c, nu: (i, 0)),
            ],
            out_specs=pl.BlockSpec(memory_space=pl.ANY),
            scratch_shapes=[pltpu.VMEM((2, TOP_K * td * SUBLANES, LANES), F32),
                            pltpu.SemaphoreType.DMA((2, nbits))],
        ),
        out_shape=jax.ShapeDtypeStruct((n_rows * SUBLANES, LANES), F32),
        compiler_params=pltpu.CompilerParams(
            dimension_semantics=("arbitrary",), vmem_limit_bytes=VMEM_LIMIT),
        name="dispatch",
    )(counts, n_used, lists, lists, pos, x2t)


def _expert_kernel(be_ref, nu_ref, xs_ref, wg_ref, wu_ref, wd_ref, ys_ref, *, tm):
    del be_ref
    d = wg_ref.shape[1]

    @pl.when(pl.program_id(0) < nu_ref[0])
    def _():
        xb = _from_token_tiles(xs_ref, tm, d).astype(BF16)
        g = _dot(xb, wg_ref[0].astype(BF16))
        u = _dot(xb, wu_ref[0].astype(BF16))
        h = g * jax.nn.sigmoid(g) * u
        _to_token_tiles(ys_ref, _dot(h.astype(BF16), wd_ref[0].astype(BF16)), tm)

    @pl.when(pl.program_id(0) >= nu_ref[0])
    def _():
        ys_ref[...] = jnp.zeros(ys_ref.shape, ys_ref.dtype)


def _experts(block_expert, n_used, xs, wg, wu, wd, tm):
    ne, d, f = wg.shape
    nb = xs.shape[0] // (tm * SUBLANES)
    rows = lambda i, be, nu: (i, 0)
    wsel = lambda i, be, nu: (be[i], 0, 0)
    return pl.pallas_call(
        functools.partial(_expert_kernel, tm=tm),
        grid_spec=pltpu.PrefetchScalarGridSpec(
            num_scalar_prefetch=2,
            grid=(nb,),
            in_specs=[
                pl.BlockSpec((tm * SUBLANES, LANES), rows),
                pl.BlockSpec((1, d, f), wsel),
                pl.BlockSpec((1, d, f), wsel),
                pl.BlockSpec((1, f, d), wsel),
            ],
            out_specs=pl.BlockSpec((tm * SUBLANES, LANES), rows),
        ),
        out_shape=jax.ShapeDtypeStruct(xs.shape, F32),
        compiler_params=pltpu.CompilerParams(
            dimension_semantics=("arbitrary",), vmem_limit_bytes=VMEM_LIMIT),
        name="experts",
    )(block_expert, n_used, xs, wg, wu, wd)


def _combine_kernel(cnt_ref, list_ref, next_list_ref, pos_ref, gate_ref, x2_ref, ln_ref, ys_ref, o_ref,
                    ybuf_ref, rbuf_ref, sems, *, tc, nbits):
    w = pl.program_id(0)
    d = o_ref.shape[1]
    slot = w % 2
    ycur_ref = ybuf_ref.at[slot]

    @pl.when(w == 0)
    def _():
        _start_copies(cnt_ref, w, list_ref, ys_ref, ycur_ref, lambda b: sems.at[slot, b], False, nbits)

    @pl.when(w + 1 < pl.num_programs(0))
    def _():
        _start_copies(cnt_ref, w + 1, next_list_ref, ys_ref, ybuf_ref.at[1 - slot],
                      lambda b: sems.at[1 - slot, b], False, nbits)

    _wait_copies(cnt_ref, w, ys_ref, ycur_ref, lambda b: sems.at[slot, b], False, nbits)

    def gather(i, c):
        for u in range(TOKEN_UNROLL):
            t = i * TOKEN_UNROLL + u
            rows = pl.ds(pl.multiple_of(t * SUBLANES, SUBLANES), SUBLANES)
            moe = None
            for k in range(TOP_K):
                p = pos_ref[0, 0, k * tc + t]
                y = ycur_ref[pl.ds(pl.multiple_of(p * SUBLANES, SUBLANES), SUBLANES), :]
                term = gate_ref[0, 0, k * tc + t] * y
                moe = term if moe is None else moe + term
            rbuf_ref[rows, :] = ALPHA * x2_ref[rows, :] + moe
        return c

    lax.fori_loop(0, tc // TOKEN_UNROLL, gather, 0)
    o_ref[...] = _layer_norm(_from_token_tiles(rbuf_ref, tc, d), ln_ref[0:1, :], ln_ref[1:2, :])


def _combine(counts, lists, pos, gates, x2t, ln3, ys, tc):
    nt = pos.shape[0]
    d = ln3.shape[1]
    nbits = counts.shape[1]
    list_len = lists.shape[2]
    smem_tile = pl.BlockSpec((1, 1, TOP_K * tc), lambda i, c: (i, 0, 0), memory_space=pltpu.SMEM)
    return pl.pallas_call(
        functools.partial(_combine_kernel, tc=tc, nbits=nbits),
        grid_spec=pltpu.PrefetchScalarGridSpec(
            num_scalar_prefetch=1,
            grid=(nt,),
            in_specs=[
                pl.BlockSpec((1, 1, list_len), lambda i, c: (i, 0, 0), memory_space=pltpu.SMEM),
                pl.BlockSpec((1, 1, list_len), lambda i, c: (jnp.minimum(i + 1, nt - 1), 0, 0),
                             memory_space=pltpu.SMEM),
                smem_tile,
                smem_tile,
                pl.BlockSpec((tc * SUBLANES, LANES), lambda i, c: (i, 0)),
                pl.BlockSpec(ln3.shape, lambda i, c: (0, 0)),
                pl.BlockSpec(memory_space=pl.ANY),
            ],
            out_specs=pl.BlockSpec((tc, d), lambda i, c: (i, 0)),
            scratch_shapes=[pltpu.VMEM((2, TOP_K * tc * SUBLANES, LANES), F32),
                            pltpu.VMEM((tc * SUBLANES, LANES), F32),
                            pltpu.SemaphoreType.DMA((2, nbits))],
        ),
        out_shape=jax.ShapeDtypeStruct((nt * tc, d), F32),
        compiler_params=pltpu.CompilerParams(
            dimension_semantics=("arbitrary",), vmem_limit_bytes=VMEM_LIMIT),
        name="combine",
    )(counts, lists, lists, pos, gates, x2t, ln3, ys)


def _routing_plan(route, wcnt, tm):
    nt, _, ts = route.shape
    nbits = ts.bit_length()
    cw = wcnt[:, :, 0].astype(jnp.int32)
    cnt = jnp.sum(cw, axis=0)
    padded = (cnt + tm - 1) // tm * tm
    pend = jnp.cumsum(padded)
    pstart = pend - padded
    base = jnp.cumsum(cw, axis=0) - cw
    offs = jnp.cumsum(cw, axis=1) - cw
    c_all = jnp.concatenate([cw, (padded - cnt)[None, :]], axis=0)
    o_all = jnp.concatenate([offs, jnp.zeros_like(cnt)[None, :]], axis=0)
    d_all = jnp.concatenate([pstart[None, :] + base, (pstart + cnt)[None, :]], axis=0)
    j = jnp.arange(N_EXPERTS, dtype=jnp.int32)
    per_class, counts = [], []
    for b in range(nbits):
        has = (c_all >> b) & 1
        done = (c_all >> (b + 1)) << (b + 1)
        slot = jnp.cumsum(has, axis=1) - has
        pick = ((slot[:, :, None] == j[None, None, :]) & (has[:, :, None] == 1)).astype(jnp.int32)
        per_class.append(jnp.einsum('wej,we->wj', pick, o_all + done))
        per_class.append(jnp.einsum('wej,we->wj', pick, d_all + done))
        counts.append(jnp.sum(has, axis=1))
    lists = jnp.concatenate(per_class, axis=1).reshape(nt + 1, 1, nbits * 2 * N_EXPERTS)
    counts = jnp.stack(counts, axis=1)
    pos = route[:, 2:4, :].astype(jnp.int32).reshape(nt, 1, TOP_K * ts)
    gates = route[:, 4:6, :].reshape(nt, 1, TOP_K * ts)
    nb = nt * ts * TOP_K // tm + N_EXPERTS
    first_row = jnp.arange(nb, dtype=jnp.int32) * tm
    block_expert = jnp.minimum(
        jnp.sum((pend[None, :] <= first_row[:, None]).astype(jnp.int32), axis=1), N_EXPERTS - 1)
    n_used = (pend[-1:] // tm).astype(jnp.int32)
    return lists, counts, pos, gates, block_expert, n_used, nb


def _moe(x2t, route, wcnt, wg, wu, wd, ln3, tm):
    ts = route.shape[2]
    lists, counts, pos, gates, block_expert, n_used, nb = _routing_plan(route, wcnt, tm)
    xs = _dispatch(counts, n_used, lists, pos, x2t, nb * tm, ts, tm)
    ys = _experts(block_expert, n_used, xs, wg, wu, wd, tm)
    return _combine(counts, lists, pos, gates, x2t, ln3, ys, ts)


def kernel(x, mem, w_in, conv_w, gn_conv, gn_sb, w_out, ln1_g, ln1_b, w_mq, w_mk, w_mv, w_mo,
           ln2_g, ln2_b, w_router_group, b_router_group, w_router_expert, b_router_expert,
           w_gate, w_up, w_down, ln3_g, ln3_b):
    for l in range(DEPTH):
        x = _layer(x, mem, w_in[l], conv_w[l], gn_conv[l], gn_sb[l], w_out[l], ln1_g[l], ln1_b[l],
                   w_mq[l], w_mk[l], w_mv[l], w_mo[l], ln2_g[l], ln2_b[l],
                   w_router_group[l], b_router_group[l], w_router_expert[l], b_router_expert[l],
                   w_gate[l], w_up[l], w_down[l], ln3_g[l], ln3_b[l])
    return x


def _layer(x, mem, w_in, conv_w, gn_conv, gn_sb, w_out, ln1_g, ln1_b, w_mq, w_mk, w_mv, w_mo,
           ln2_g, ln2_b, w_rg, b_rg, w_re, b_re, w_gate, w_up, w_down, ln3_g, ln3_b,
           ts=512, tq=512, tm=512, kv_nb=4):
    bn, sl, d = x.shape
    bf = lambda w: w.astype(BF16)
    yc, q, k, v = _proj_conv(x, bf(w_in), conv_w, gn_conv, ts)
    ysb = _sb_attention(q, k, v, gn_sb, tq)
    kb, vb = _kv_proj(mem, bf(w_mk), bf(w_mv), kv_nb)
    ln, wr_t, br = _router_params(ln1_g, ln1_b, ln2_g, ln2_b, w_rg, b_rg, w_re, b_re)
    x2t, route, wcnt = _post_mixer(x, yc, ysb, kb, vb, bf(w_out), bf(w_mq), bf(w_mo),
                                   ln, wr_t, br, ts)
    ln3 = jnp.stack([ln3_g, ln3_b]).astype(F32)
    out = _moe(x2t, route, wcnt, w_gate, w_up, w_down, ln3, tm)
    return out.reshape(bn, sl, d)
```

```python
import functools

import jax
import jax.numpy as jnp
import numpy as np
from jax import lax
from jax.experimental import pallas as pl
from jax.experimental.pallas import tpu as pltpu

CONV_WIDTH = 512
CONV_GROUP = 64
CONV_K = 3
SB_HEADS = 8
SB_HEAD_DIM = 64
SB_WIDTH = SB_HEADS * SB_HEAD_DIM
MEM_HEADS = 4
N_GROUPS = 4
EXPERTS_PER_GROUP = 8
N_EXPERTS = N_GROUPS * EXPERTS_PER_GROUP
TOP_K = 2
DEPTH = 1
ALPHA = (2 * DEPTH) ** 0.25
LN_EPS = 1e-5
LOG2E = 1.4426950408889634
RMS_EPS = 1e-6

LANES = 128
SUBLANES = 8
VMEM_LIMIT = 56 * 1024 * 1024

BF16 = jnp.bfloat16
F32 = jnp.float32


def _split_bf16(a):
    hi = a.astype(BF16)
    lo = (a - hi.astype(F32)).astype(BF16)
    return hi, lo


def _dot(a, b):
    return jnp.dot(a, b, preferred_element_type=F32)


def _dot_nt(a, b):
    return lax.dot_general(a, b, (((1,), (1,)), ((), ())), preferred_element_type=F32)


def _layer_norm(r, g, b):
    mu = jnp.mean(r, axis=-1, keepdims=True)
    c = r - mu
    var = jnp.mean(c * c, axis=-1, keepdims=True)
    return c * lax.rsqrt(var + LN_EPS) * g + b


def _proj_conv_kernel(x_ref, w_ref, cw_ref, gn_ref, gmat_ref,
                      yc_ref, q_ref, k_ref, v_ref, ubuf_ref, *, ts):
    s = pl.program_id(1)
    xb = x_ref[0].astype(BF16)
    cw = CONV_WIDTH

    @pl.when(s == 0)
    def _():
        ubuf_ref[0:SUBLANES, :] = jnp.zeros((SUBLANES, cw), F32)

    cc = _dot(xb, w_ref[:, cw:2 * cw])
    ch = _dot(xb, w_ref[:, 2 * cw:3 * cw])
    u = cc * ch
    ubuf_ref[SUBLANES:SUBLANES + ts, :] = u
    um1 = ubuf_ref[SUBLANES - 1:SUBLANES - 1 + ts, :]
    um2 = ubuf_ref[SUBLANES - 2:SUBLANES - 2 + ts, :]
    conv = cw_ref[2:3, :] * u + cw_ref[1:2, :] * um1 + cw_ref[0:1, :] * um2
    ubuf_ref[0:SUBLANES, :] = ubuf_ref[ts:ts + SUBLANES, :]

    cb = _dot(xb, w_ref[:, 0:cw])
    y = cb * conv
    hi, lo = _split_bf16(y * y)
    gsum = _dot(hi, gmat_ref[...]) + _dot(lo, gmat_ref[...])
    yn = y * lax.rsqrt(gsum * (1.0 / CONV_GROUP) + RMS_EPS) * gn_ref[...]
    yc_ref[0] = yn.astype(yc_ref.dtype)

    sw = SB_WIDTH
    base = 3 * cw
    q_ref[0] = (_dot(xb, w_ref[:, base:base + sw]) * (SB_HEAD_DIM ** -0.5 * LOG2E)).astype(q_ref.dtype)
    k_ref[0] = _dot(xb, w_ref[:, base + sw:base + 2 * sw]).astype(k_ref.dtype)
    v_ref[0] = _dot(xb, w_ref[:, base + 2 * sw:base + 3 * sw]).astype(v_ref.dtype)


def _proj_conv(x, w_in_b, conv_w, gn_conv, ts):
    bn, sl, d = x.shape
    gidx = np.arange(CONV_WIDTH) // CONV_GROUP
    gmat = jnp.asarray(gidx[:, None] == gidx[None, :], BF16)
    outs = [jax.ShapeDtypeStruct((bn, sl, CONV_WIDTH), BF16)] * 4
    tile = lambda w: pl.BlockSpec((1, ts, w), lambda b, s: (b, s, 0))
    full = lambda a: pl.BlockSpec(a.shape, lambda b, s: (0,) * a.ndim)
    gn = gn_conv.reshape(1, CONV_WIDTH)
    return pl.pallas_call(
        functools.partial(_proj_conv_kernel, ts=ts),
        grid=(bn, sl // ts),
        in_specs=[tile(d), full(w_in_b), full(conv_w), full(gn), full(gmat)],
        out_specs=[tile(CONV_WIDTH)] * 4,
        out_shape=outs,
        scratch_shapes=[pltpu.VMEM((ts + 2 * SUBLANES, CONV_WIDTH), F32)],
        compiler_params=pltpu.CompilerParams(
            dimension_semantics=("arbitrary", "arbitrary"), vmem_limit_bytes=VMEM_LIMIT),
        name="proj_conv",
    )(x, w_in_b, conv_w, gn, gmat)


SB_TK = 128
SB_UNROLL = 2
SB_ROW_GROUPS = 2
SB_PAIRS = 4
SB_DEAD = 192.0
BF16_BITS_OF_F32 = 0xFFFF0000


def _bits(a):
    return lax.bitcast_convert_type(a, jnp.uint32)


def _softplus2(z2):
    pos = jnp.maximum(z2, 0.0)
    return pos + jnp.log2(1.0 + jnp.exp2(z2 - 2.0 * pos))


def _sb_kernel(q_ref, k_ref, v_ref, gn_ref, cum_ref, o_ref, acc_ref, run_ref, *, tq):
    qi = pl.program_id(2)
    tk = SB_TK
    nd = tq // tk
    lane = lax.broadcasted_iota(jnp.int32, (1, LANES), 1)
    head0 = lane < SB_HEAD_DIM
    zero = jnp.zeros((), BF16)
    lanes_of = lambda p: slice(p * LANES, (p + 1) * LANES)

    def per_head_rows(ref, p, kstart):
        blk = ref[0, pl.ds(kstart, tk), lanes_of(p)]
        return jnp.concatenate([jnp.where(head0, blk, zero), jnp.where(head0, zero, blk)], axis=0)

    def tile(p, row0, row1, kstart, diag, runs):
        rows = row1 - row0
        z2 = _dot_nt(q_ref[0, row0:row1, lanes_of(p)], per_head_rows(k_ref, p, kstart))
        if diag:
            t_loc = lax.broadcasted_iota(jnp.int32, (tk, tk), 0)
            s_loc = lax.broadcasted_iota(jnp.int32, (tk, tk), 1)
            before = s_loc < t_loc
            causal = lambda m: jnp.concatenate(
                [jnp.where(before, m[:tk], 0.0)] + ([m[tk:]] if rows > tk else []), axis=0)
        a_parts, new_runs = [], []
        for h in range(2):
            zh = z2[:, h * tk:(h + 1) * tk]
            sp = _softplus2(zh)
            if diag:
                sp = causal(sp)
            hi = lax.bitcast_convert_type(_bits(sp) & jnp.uint32(BF16_BITS_OF_F32), F32)
            lhs = jnp.concatenate([hi.astype(BF16), (sp - hi).astype(BF16)], axis=1)
            cs = _dot(lhs, cum_ref[...])
            a = jnp.exp2(zh - cs[:, :tk] - runs[h])
            if diag:
                a = causal(a)
            a_parts.append(a.astype(BF16))
            new_runs.append(runs[h] + cs[:, tk:])
        return jnp.concatenate(a_parts, axis=1), new_runs

    acc = [None] * SB_PAIRS
    runs = [None] * SB_PAIRS
    for d in reversed(range(nd)):
        row0 = d * tk
        kstart = pl.multiple_of(qi * tq + row0, tk)
        fresh = jnp.zeros((tk, LANES), F32)
        for p in range(SB_PAIRS):
            grown = ([fresh, fresh] if runs[p] is None
                     else [jnp.concatenate([fresh, r], axis=0) for r in runs[p]])
            a, runs[p] = tile(p, row0, tq, kstart, True, grown)
            av = _dot(a, per_head_rows(v_ref, p, kstart))
            acc[p] = av if acc[p] is None else av + jnp.concatenate([fresh, acc[p]], axis=0)
    for p in range(SB_PAIRS):
        acc_ref[p] = acc[p]
        run_ref[p, 0] = runs[p][0]
        run_ref[p, 1] = runs[p][1]

    n_iter = qi * nd // SB_UNROLL

    def all_dead(runs):
        lowest = functools.reduce(jnp.minimum, [r for pair in runs for r in pair])
        return (jnp.min(lowest) > SB_DEAD).astype(jnp.int32)

    group = tq // SB_ROW_GROUPS
    for g in range(SB_ROW_GROUPS):
        r0, r1 = g * group, (g + 1) * group
        load_runs = lambda r0=r0, r1=r1: [[run_ref[p, h, r0:r1, :] for h in range(2)]
                                          for p in range(SB_PAIRS)]

        def more(carry):
            j, dead = carry
            return jnp.logical_and(j < n_iter, dead == 0)

        def body(carry, r0=r0, r1=r1, load_runs=load_runs):
            j, _ = carry
            runs = load_runs()
            a_parts = [[] for _ in range(SB_PAIRS)]
            v_parts = [[] for _ in range(SB_PAIRS)]
            for u in range(SB_UNROLL):
                kstart = pl.multiple_of((qi * nd - 1 - (j * SB_UNROLL + u)) * tk, tk)
                for p in range(SB_PAIRS):
                    a, runs[p] = tile(p, r0, r1, kstart, False, runs[p])
                    a_parts[p].append(a)
                    v_parts[p].append(per_head_rows(v_ref, p, kstart))
            for p in range(SB_PAIRS):
                acc_ref[p, r0:r1, :] += _dot(jnp.concatenate(a_parts[p], axis=1),
                                             jnp.concatenate(v_parts[p], axis=0))
                run_ref[p, 0, r0:r1, :] = runs[p][0]
                run_ref[p, 1, r0:r1, :] = runs[p][1]
            return j + 1, all_dead(runs)

        lax.while_loop(more, body, (jnp.int32(0), all_dead(load_runs())))

    for p in range(SB_PAIRS):
        y = acc_ref[p]
        y2 = y * y
        s0 = jnp.sum(jnp.where(head0, y2, 0.0), axis=-1, keepdims=True)
        s1 = jnp.sum(jnp.where(head0, 0.0, y2), axis=-1, keepdims=True)
        ms = jnp.where(head0, s0, s1) * (1.0 / SB_HEAD_DIM)
        o_ref[0, :, lanes_of(p)] = (y * lax.rsqrt(ms + RMS_EPS) * gn_ref[:, lanes_of(p)]).astype(o_ref.dtype)


def _sb_attention(q, k, v, gn_sb, tq):
    bn, sl, w = q.shape
    wide = SB_PAIRS * LANES
    j = np.arange(SB_TK)
    assert (tq // SB_TK) % SB_UNROLL == 0 and w % wide == 0
    cum = np.concatenate([(j[:, None] >= j[None, :]), np.ones((SB_TK, SB_TK), bool)], axis=1)
    cum = jnp.asarray(np.concatenate([cum, cum], axis=0), BF16)
    gn = gn_sb.reshape(1, w)
    return pl.pallas_call(
        functools.partial(_sb_kernel, tq=tq),
        grid=(bn, w // wide, sl // tq),
        in_specs=[
            pl.BlockSpec((1, tq, wide), lambda b, p, i: (b, i, p)),
            pl.BlockSpec((1, sl, wide), lambda b, p, i: (b, 0, p)),
            pl.BlockSpec((1, sl, wide), lambda b, p, i: (b, 0, p)),
            pl.BlockSpec((1, wide), lambda b, p, i: (0, p)),
            pl.BlockSpec(cum.shape, lambda b, p, i: (0, 0)),
        ],
        out_specs=pl.BlockSpec((1, tq, wide), lambda b, p, i: (b, i, p)),
        out_shape=jax.ShapeDtypeStruct((bn, sl, w), BF16),
        scratch_shapes=[pltpu.VMEM((SB_PAIRS, tq, LANES), F32), pltpu.VMEM((SB_PAIRS, 2, tq, LANES), F32)],
        compiler_params=pltpu.CompilerParams(
            dimension_semantics=("arbitrary", "arbitrary", "arbitrary"),
            vmem_limit_bytes=VMEM_LIMIT),
        name="sb_attention",
    )(q, k, v, gn, cum)


def _kv_kernel(mem_ref, wk_ref, wv_ref, k_ref, v_ref):
    nb, ml, d = mem_ref.shape
    m = mem_ref[...].reshape(nb * ml, d).astype(BF16)
    k_ref[...] = _dot(m, wk_ref[...]).reshape(nb, ml, d).astype(k_ref.dtype)
    v_ref[...] = _dot(m, wv_ref[...]).reshape(nb, ml, d).astype(v_ref.dtype)


def _kv_proj(mem, wk_b, wv_b, nb):
    bn, ml, d = mem.shape
    blk = pl.BlockSpec((nb, ml, d), lambda i: (i, 0, 0))
    full = pl.BlockSpec((d, d), lambda i: (0, 0))
    return pl.pallas_call(
        _kv_kernel,
        grid=(bn // nb,),
        in_specs=[blk, full, full],
        out_specs=[blk, blk],
        out_shape=[jax.ShapeDtypeStruct((bn, ml, d), BF16)] * 2,
        compiler_params=pltpu.CompilerParams(
            dimension_semantics=("arbitrary",), vmem_limit_bytes=VMEM_LIMIT),
        name="kv_proj",
    )(mem, wk_b, wv_b)


ROUTE_ROWS = 8
ROUTER_ROWS = 40


def _to_token_tiles(ref, val, ts):
    for c in range(val.shape[1] // LANES):
        ref[pl.ds(c, ts, stride=SUBLANES), :] = val[:, c * LANES:(c + 1) * LANES]


def _from_token_tiles(ref, ts, d):
    return jnp.concatenate(
        [ref[pl.ds(c, ts, stride=SUBLANES), :] for c in range(d // LANES)], axis=1)


def _post_kernel(x_ref, yc_ref, ysb_ref, kb_ref, vb_ref, wo_ref, wq_ref, wmo_ref,
                 ln_ref, wr_ref, br_ref, tri_ref, lt_ref,
                 x2_ref, route_ref, cnt_ref, *, ts):
    d = x_ref.shape[2]
    cw = yc_ref.shape[2]
    x = x_ref[0]
    h1 = _dot(yc_ref[0], wo_ref[0:cw, :]) + _dot(ysb_ref[0], wo_ref[cw:, :])
    x1 = _layer_norm(ALPHA * x + h1, ln_ref[0:1, :], ln_ref[1:2, :])

    hd = d // MEM_HEADS
    q = (_dot(x1.astype(BF16), wq_ref[...]) * (hd ** -0.5)).astype(BF16)
    outs = []
    for h in range(MEM_HEADS):
        sc = _dot_nt(q[:, h * hd:(h + 1) * hd], kb_ref[0, :, h * hd:(h + 1) * hd])
        p = jnp.exp(sc - jnp.max(sc, axis=-1, keepdims=True))
        den = jnp.sum(p, axis=-1, keepdims=True)
        o = _dot(p.astype(BF16), vb_ref[0, :, h * hd:(h + 1) * hd])
        outs.append((o * (1.0 / den)).astype(BF16))
    o = jnp.concatenate(outs, axis=1)
    h2 = _dot(o, wmo_ref[...])
    x2 = _layer_norm(ALPHA * x1 + h2, ln_ref[2:3, :], ln_ref[3:4, :])
    _to_token_tiles(x2_ref, x2, ts)

    lt = _dot_nt(wr_ref[...], x2.astype(BF16)) + br_ref[:, 0:1]
    gl = [lt[g:g + 1, :] for g in range(N_GROUPS)]
    gmax = functools.reduce(jnp.maximum, gl)
    gden = functools.reduce(lambda a, b: a + b, [jnp.exp(l - gmax) for l in gl])
    gprob = 1.0 / gden
    grp = jnp.full(gmax.shape, N_GROUPS - 1, jnp.int32)
    for g in reversed(range(N_GROUPS - 1)):
        grp = jnp.where(gl[g] >= gmax, g, grp)
    el = lt[N_GROUPS:N_GROUPS + EXPERTS_PER_GROUP, :]
    for g in range(1, N_GROUPS):
        lo = N_GROUPS + g * EXPERTS_PER_GROUP
        el = jnp.where(grp == g, lt[lo:lo + EXPERTS_PER_GROUP, :], el)
    row = lax.broadcasted_iota(jnp.int32, el.shape, 0)
    t1 = jnp.max(el, axis=0, keepdims=True)
    i1 = jnp.min(jnp.where(el >= t1, row, EXPERTS_PER_GROUP), axis=0, keepdims=True)
    el2 = jnp.where(row == i1, -jnp.inf, el)
    t2 = jnp.max(el2, axis=0, keepdims=True)
    i2 = jnp.min(jnp.where(el2 >= t2, row, EXPERTS_PER_GROUP), axis=0, keepdims=True)
    e2w = jnp.exp(t2 - t1)
    g0 = gprob / (1.0 + e2w)
    g1 = gprob * e2w / (1.0 + e2w)
    e0 = grp * EXPERTS_PER_GROUP + i1
    e1 = grp * EXPERTS_PER_GROUP + i2

    erow = lax.broadcasted_iota(jnp.int32, (N_EXPERTS, ts), 0)
    oh0 = erow == e0
    oh1 = erow == e1
    both = jnp.where(jnp.logical_or(oh0, oh1), 1.0, 0.0).astype(BF16)
    tot = _dot(both, tri_ref[...])
    lower = _dot(lt_ref[...], both).astype(BF16)
    offs = _dot(lower, tri_ref[:, ts:])
    pos = tot[:, :ts] + offs[:, 0:1]
    p0 = jnp.sum(jnp.where(oh0, pos, 0.0), axis=0, keepdims=True)
    p1 = jnp.sum(jnp.where(oh1, pos, 0.0), axis=0, keepdims=True)
    cnt_ref[0] = tot[:, ts:]

    zero = jnp.zeros_like(g0)
    route_ref[0] = jnp.concatenate(
        [e0.astype(F32), e1.astype(F32), p0, p1, g0, g1, zero, zero], axis=0)


def _post_mixer(x, yc, ysb, kb, vb, wo_b, wq_b, wmo_b, ln, wr_t, br, ts):
    bn, sl, d = x.shape
    cw = yc.shape[2]
    ml = kb.shape[1]
    nt = sl // ts
    j = np.arange(ts)
    tri = jnp.asarray(np.concatenate([j[:, None] < j[None, :], np.ones((ts, LANES), bool)], axis=1), BF16)
    je = np.arange(N_EXPERTS)
    lt = jnp.asarray(je[None, :] < je[:, None], BF16)
    tile = lambda w: pl.BlockSpec((1, ts, w), lambda b, s: (b, s, 0))
    full = lambda a: pl.BlockSpec(a.shape, lambda b, s: (0,) * a.ndim)
    kvspec = pl.BlockSpec((1, ml, d), lambda b, s: (b, 0, 0))
    return pl.pallas_call(
        functools.partial(_post_kernel, ts=ts),
        grid=(bn, nt),
        in_specs=[tile(d), tile(cw), tile(ysb.shape[2]), kvspec, kvspec,
                  full(wo_b), full(wq_b), full(wmo_b), full(ln), full(wr_t), full(br), full(tri),
                  full(lt)],
        out_specs=[
            pl.BlockSpec((ts * SUBLANES, LANES), lambda b, s: (b * nt + s, 0)),
            pl.BlockSpec((1, ROUTE_ROWS, ts), lambda b, s: (b * nt + s, 0, 0)),
            pl.BlockSpec((1, N_EXPERTS, LANES), lambda b, s: (b * nt + s, 0, 0)),
        ],
        out_shape=[
            jax.ShapeDtypeStruct((bn * sl * SUBLANES, LANES), F32),
            jax.ShapeDtypeStruct((bn * nt, ROUTE_ROWS, ts), F32),
            jax.ShapeDtypeStruct((bn * nt, N_EXPERTS, LANES), F32),
        ],
        compiler_params=pltpu.CompilerParams(
            dimension_semantics=("arbitrary", "arbitrary"), vmem_limit_bytes=VMEM_LIMIT),
        name="post_mixer",
    )(x, yc, ysb, kb, vb, wo_b, wq_b, wmo_b, ln, wr_t, br, tri, lt)


def _router_params(ln1_g, ln1_b, ln2_g, ln2_b, w_rg, b_rg, w_re, b_re):
    d = w_rg.shape[0]
    ln = jnp.stack([ln1_g, ln1_b, ln2_g, ln2_b]).astype(F32)
    pad = ROUTER_ROWS - N_GROUPS - N_EXPERTS
    wr_t = jnp.concatenate([w_rg.T, w_re.T, jnp.zeros((pad, d), w_rg.dtype)], axis=0).astype(BF16)
    br = jnp.concatenate([b_rg, b_re, jnp.zeros((pad,), b_rg.dtype)]).astype(F32)
    br = jnp.broadcast_to(br[:, None], (ROUTER_ROWS, LANES))
    return ln, wr_t, br


def _start_copies(cnt_ref, win, list_ref, hbm_ref, buf_ref, sem_of, to_hbm, nbits):
    for b in range(nbits):
        n = (1 << b) * SUBLANES

        def one(i, c, b=b, n=n):
            loc_tok = list_ref[0, 0, b * 2 * N_EXPERTS + i]
            hbm_tok = list_ref[0, 0, (b * 2 + 1) * N_EXPERTS + i]
            loc = buf_ref.at[pl.ds(pl.multiple_of(loc_tok * SUBLANES, SUBLANES), n), :]
            glob = hbm_ref.at[pl.ds(pl.multiple_of(hbm_tok * SUBLANES, SUBLANES), n), :]
            if to_hbm:
                pltpu.make_async_copy(loc, glob, sem_of(b)).start()
            else:
                pltpu.make_async_copy(glob, loc, sem_of(b)).start()
            return c

        lax.fori_loop(0, cnt_ref[win, b], one, 0)


def _wait_copies(cnt_ref, win, hbm_ref, buf_ref, sem_of, to_hbm, nbits):
    for b in range(nbits):
        n = (1 << b) * SUBLANES

        def one(i, c, b=b, n=n):
            loc = buf_ref.at[0:n, :]
            glob = hbm_ref.at[0:n, :]
            if to_hbm:
                pltpu.make_async_copy(loc, glob, sem_of(b)).wait()
            else:
                pltpu.make_async_copy(glob, loc, sem_of(b)).wait()
            return c

        lax.fori_loop(0, cnt_ref[win, b], one, 0)


TOKEN_UNROLL = 8


def _dispatch_kernel(cnt_ref, nu_ref, list_ref, pad_list_ref, pos_ref, x2_ref, xs_ref, sbuf_ref, sems,
                     *, td, tm, nbits):
    w = pl.program_id(0)
    last = pl.num_programs(0) - 1
    slot = w % 2
    scur_ref = sbuf_ref.at[slot]

    @pl.when(w == 0)
    def _():
        block = tm * SUBLANES
        zeros_ref = sbuf_ref.at[1]
        zeros_ref[0:block, :] = jnp.zeros((block, LANES), F32)
        pad_row = pl.num_programs(0)
        sem_of = lambda b: sems.at[1, b]
        _start_copies(cnt_ref, pad_row, pad_list_ref, xs_ref, zeros_ref, sem_of, True, nbits)
        _wait_copies(cnt_ref, pad_row, xs_ref, zeros_ref, sem_of, True, nbits)
        tail = lambda i: pltpu.make_async_copy(
            zeros_ref.at[0:block, :], xs_ref.at[pl.ds(pl.multiple_of(i * block, block), block), :],
            sems.at[1, 0])
        n_blocks = xs_ref.shape[0] // block
        lax.fori_loop(nu_ref[0], n_blocks, lambda i, c: (tail(i).start(), c)[1], 0)
        lax.fori_loop(nu_ref[0], n_blocks, lambda i, c: (tail(i).wait(), c)[1], 0)

    def place(i, c):
        for u in range(TOKEN_UNROLL):
            t = i * TOKEN_UNROLL + u
            tok = x2_ref[pl.ds(pl.multiple_of(t * SUBLANES, SUBLANES), SUBLANES), :]
            for k in range(TOP_K):
                p = pos_ref[0, 0, k * td + t]
                scur_ref[pl.ds(pl.multiple_of(p * SUBLANES, SUBLANES), SUBLANES), :] = tok
        return c

    lax.fori_loop(0, td // TOKEN_UNROLL, place, 0)
    _start_copies(cnt_ref, w, list_ref, xs_ref, scur_ref, lambda b: sems.at[slot, b], True, nbits)

    @pl.when(w > 0)
    def _():
        _wait_copies(cnt_ref, w - 1, xs_ref, sbuf_ref.at[1 - slot], lambda b: sems.at[1 - slot, b],
                     True, nbits)

    @pl.when(w == last)
    def _():
        _wait_copies(cnt_ref, w, xs_ref, scur_ref, lambda b: sems.at[slot, b], True, nbits)


def _dispatch(counts, n_used, lists, pos, x2t, n_rows, td, tm):
    nt = pos.shape[0]
    nbits = counts.shape[1]
    assert tm <= TOP_K * td and n_rows % tm == 0 and nbits == td.bit_length()
    list_len = lists.shape[2]
    return pl.pallas_call(
        functools.partial(_dispatch_kernel, td=td, tm=tm, nbits=nbits),
        grid_spec=pltpu.PrefetchScalarGridSpec(
            num_scalar_prefetch=2,
            grid=(nt,),
            in_specs=[
                pl.BlockSpec((1, 1, list_len), lambda i, c, nu: (i, 0, 0), memory_space=pltpu.SMEM),
                pl.BlockSpec((1, 1, list_len), lambda i, c, nu: (nt, 0, 0), memory_space=pltpu.SMEM),
                pl.BlockSpec((1, 1, TOP_K * td), lambda i, c, nu: (i, 0, 0), memory_space=pltpu.SMEM),
                pl.BlockSpec((td * SUBLANES, LANES), lambda i, c, nu: (i, 0)),
            ],
            out_specs=pl.BlockSpec(memory_space=pl.ANY),
            scratch_shapes=[pltpu.VMEM((2, TOP_K * td * SUBLANES, LANES), F32),
                            pltpu.SemaphoreType.DMA((2, nbits))],
        ),
        out_shape=jax.ShapeDtypeStruct((n_rows * SUBLANES, LANES), F32),
        compiler_params=pltpu.CompilerParams(
            dimension_semantics=("arbitrary",), vmem_limit_bytes=VMEM_LIMIT),
        name="dispatch",
    )(counts, n_used, lists, lists, pos, x2t)


def _expert_kernel(be_ref, nu_ref, xs_ref, wg_ref, wu_ref, wd_ref, ys_ref, *, tm):
    del be_ref
    d = wg_ref.shape[1]

    @pl.when(pl.program_id(0) < nu_ref[0])
    def _():
        xb = _from_token_tiles(xs_ref, tm, d).astype(BF16)
        g = _dot(xb, wg_ref[0].astype(BF16))
        u = _dot(xb, wu_ref[0].astype(BF16))
        h = g * jax.nn.sigmoid(g) * u
        _to_token_tiles(ys_ref, _dot(h.astype(BF16), wd_ref[0].astype(BF16)), tm)

    @pl.when(pl.program_id(0) >= nu_ref[0])
    def _():
        ys_ref[...] = jnp.zeros(ys_ref.shape, ys_ref.dtype)


def _experts(block_expert, n_used, xs, wg, wu, wd, tm):
    ne, d, f = wg.shape
    nb = xs.shape[0] // (tm * SUBLANES)
    rows = lambda i, be, nu: (i, 0)
    wsel = lambda i, be, nu: (be[i], 0, 0)
    return pl.pallas_call(
        functools.partial(_expert_kernel, tm=tm),
        grid_spec=pltpu.PrefetchScalarGridSpec(
            num_scalar_prefetch=2,
            grid=(nb,),
            in_specs=[
                pl.BlockSpec((tm * SUBLANES, LANES), rows),
                pl.BlockSpec((1, d, f), wsel),
                pl.BlockSpec((1, d, f), wsel),
                pl.BlockSpec((1, f, d), wsel),
            ],
            out_specs=pl.BlockSpec((tm * SUBLANES, LANES), rows),
        ),
        out_shape=jax.ShapeDtypeStruct(xs.shape, F32),
        compiler_params=pltpu.CompilerParams(
            dimension_semantics=("arbitrary",), vmem_limit_bytes=VMEM_LIMIT),
        name="experts",
    )(block_expert, n_used, xs, wg, wu, wd)


def _combine_kernel(cnt_ref, list_ref, next_list_ref, pos_ref, gate_ref, x2_ref, ln_ref, ys_ref, o_ref,
                    ybuf_ref, rbuf_ref, sems, *, tc, nbits):
    w = pl.program_id(0)
    d = o_ref.shape[1]
    slot = w % 2
    ycur_ref = ybuf_ref.at[slot]

    @pl.when(w == 0)
    def _():
        _start_copies(cnt_ref, w, list_ref, ys_ref, ycur_ref, lambda b: sems.at[slot, b], False, nbits)

    @pl.when(w + 1 < pl.num_programs(0))
    def _():
        _start_copies(cnt_ref, w + 1, next_list_ref, ys_ref, ybuf_ref.at[1 - slot],
                      lambda b: sems.at[1 - slot, b], False, nbits)

    _wait_copies(cnt_ref, w, ys_ref, ycur_ref, lambda b: sems.at[slot, b], False, nbits)

    def gather(i, c):
        for u in range(TOKEN_UNROLL):
            t = i * TOKEN_UNROLL + u
            rows = pl.ds(pl.multiple_of(t * SUBLANES, SUBLANES), SUBLANES)
            moe = None
            for k in range(TOP_K):
                p = pos_ref[0, 0, k * tc + t]
                y = ycur_ref[pl.ds(pl.multiple_of(p * SUBLANES, SUBLANES), SUBLANES), :]
                term = gate_ref[0, 0, k * tc + t] * y
                moe = term if moe is None else moe + term
            rbuf_ref[rows, :] = ALPHA * x2_ref[rows, :] + moe
        return c

    lax.fori_loop(0, tc // TOKEN_UNROLL, gather, 0)
    o_ref[...] = _layer_norm(_from_token_tiles(rbuf_ref, tc, d), ln_ref[0:1, :], ln_ref[1:2, :])


def _combine(counts, lists, pos, gates, x2t, ln3, ys, tc):
    nt = pos.shape[0]
    d = ln3.shape[1]
    nbits = counts.shape[1]
    list_len = lists.shape[2]
    smem_tile = pl.BlockSpec((1, 1, TOP_K * tc), lambda i, c: (i, 0, 0), memory_space=pltpu.SMEM)
    return pl.pallas_call(
        functools.partial(_combine_kernel, tc=tc, nbits=nbits),
        grid_spec=pltpu.PrefetchScalarGridSpec(
            num_scalar_prefetch=1,
            grid=(nt,),
            in_specs=[
                pl.BlockSpec((1, 1, list_len), lambda i, c: (i, 0, 0), memory_space=pltpu.SMEM),
                pl.BlockSpec((1, 1, list_len), lambda i, c: (jnp.minimum(i + 1, nt - 1), 0, 0),
                             memory_space=pltpu.SMEM),
                smem_tile,
                smem_tile,
                pl.BlockSpec((tc * SUBLANES, LANES), lambda i, c: (i, 0)),
                pl.BlockSpec(ln3.shape, lambda i, c: (0, 0)),
                pl.BlockSpec(memory_space=pl.ANY),
            ],
            out_specs=pl.BlockSpec((tc, d), lambda i, c: (i, 0)),
            scratch_shapes=[pltpu.VMEM((2, TOP_K * tc * SUBLANES, LANES), F32),
                            pltpu.VMEM((tc * SUBLANES, LANES), F32),
                            pltpu.SemaphoreType.DMA((2, nbits))],
        ),
        out_shape=jax.ShapeDtypeStruct((nt * tc, d), F32),
        compiler_params=pltpu.CompilerParams(
            dimension_semantics=("arbitrary",), vmem_limit_bytes=VMEM_LIMIT),
        name="combine",
    )(counts, lists, lists, pos, gates, x2t, ln3, ys)


def _routing_plan(route, wcnt, tm):
    nt, _, ts = route.shape
    nbits = ts.bit_length()
    cw = wcnt[:, :, 0].astype(jnp.int32)
    cnt = jnp.sum(cw, axis=0)
    padded = (cnt + tm - 1) // tm * tm
    pend = jnp.cumsum(padded)
    pstart = pend - padded
    base = jnp.cumsum(cw, axis=0) - cw
    offs = jnp.cumsum(cw, axis=1) - cw
    c_all = jnp.concatenate([cw, (padded - cnt)[None, :]], axis=0)
    o_all = jnp.concatenate([offs, jnp.zeros_like(cnt)[None, :]], axis=0)
    d_all = jnp.concatenate([pstart[None, :] + base, (pstart + cnt)[None, :]], axis=0)
    j = jnp.arange(N_EXPERTS, dtype=jnp.int32)
    per_class, counts = [], []
    for b in range(nbits):
        has = (c_all >> b) & 1
        done = (c_all >> (b + 1)) << (b + 1)
        slot = jnp.cumsum(has, axis=1) - has
        pick = ((slot[:, :, None] == j[None, None, :]) & (has[:, :, None] == 1)).astype(jnp.int32)
        per_class.append(jnp.einsum('wej,we->wj', pick, o_all + done))
        per_class.append(jnp.einsum('wej,we->wj', pick, d_all + done))
        counts.append(jnp.sum(has, axis=1))
    lists = jnp.concatenate(per_class, axis=1).reshape(nt + 1, 1, nbits * 2 * N_EXPERTS)
    counts = jnp.stack(counts, axis=1)
    pos = route[:, 2:4, :].astype(jnp.int32).reshape(nt, 1, TOP_K * ts)
    gates = route[:, 4:6, :].reshape(nt, 1, TOP_K * ts)
    nb = nt * ts * TOP_K // tm + N_EXPERTS
    first_row = jnp.arange(nb, dtype=jnp.int32) * tm
    block_expert = jnp.minimum(
        jnp.sum((pend[None, :] <= first_row[:, None]).astype(jnp.int32), axis=1), N_EXPERTS - 1)
    n_used = (pend[-1:] // tm).astype(jnp.int32)
    return lists, counts, pos, gates, block_expert, n_used, nb


def _moe(x2t, route, wcnt, wg, wu, wd, ln3, tm):
    ts = route.shape[2]
    lists, counts, pos, gates, block_expert, n_used, nb = _routing_plan(route, wcnt, tm)
    xs = _dispatch(counts, n_used, lists, pos, x2t, nb * tm, ts, tm)
    ys = _experts(block_expert, n_used, xs, wg, wu, wd, tm)
    return _combine(counts, lists, pos, gates, x2t, ln3, ys, ts)


def kernel(x, mem, w_in, conv_w, gn_conv, gn_sb, w_out, ln1_g, ln1_b, w_mq, w_mk, w_mv, w_mo,
           ln2_g, ln2_b, w_router_group, b_router_group, w_router_expert, b_router_expert,
           w_gate, w_up, w_down, ln3_g, ln3_b):
    for l in range(DEPTH):
        x = _layer(x, mem, w_in[l], conv_w[l], gn_conv[l], gn_sb[l], w_out[l], ln1_g[l], ln1_b[l],
                   w_mq[l], w_mk[l], w_mv[l], w_mo[l], ln2_g[l], ln2_b[l],
                   w_router_group[l], b_router_group[l], w_router_expert[l], b_router_expert[l],
                   w_gate[l], w_up[l], w_down[l], ln3_g[l], ln3_b[l])
    return x


def _layer(x, mem, w_in, conv_w, gn_conv, gn_sb, w_out, ln1_g, ln1_b, w_mq, w_mk, w_mv, w_mo,
           ln2_g, ln2_b, w_rg, b_rg, w_re, b_re, w_gate, w_up, w_down, ln3_g, ln3_b,
           ts=512, tq=512, tm=512, kv_nb=4):
    bn, sl, d = x.shape
    bf = lambda w: w.astype(BF16)
    yc, q, k, v = _proj_conv(x, bf(w_in), conv_w, gn_conv, ts)
    ysb = _sb_attention(q, k, v, gn_sb, tq)
    kb, vb = _kv_proj(mem, bf(w_mk), bf(w_mv), kv_nb)
    ln, wr_t, br = _router_params(ln1_g, ln1_b, ln2_g, ln2_b, w_rg, b_rg, w_re, b_re)
    x2t, route, wcnt = _post_mixer(x, yc, ysb, kb, vb, bf(w_out), bf(w_mq), bf(w_mo),
                                   ln, wr_t, br, ts)
    ln3 = jnp.stack([ln3_g, ln3_b]).astype(F32)
    out = _moe(x2t, route, wcnt, w_gate, w_up, w_down, ln3, tm)
    return out.reshape(bn, sl, d)
```

```python
import functools

import jax
import jax.numpy as jnp
import numpy as np
from jax import lax
from jax.experimental import pallas as pl
from jax.experimental.pallas import tpu as pltpu

CONV_WIDTH = 512
CONV_GROUP = 64
CONV_K = 3
SB_HEADS = 8
SB_HEAD_DIM = 64
SB_WIDTH = SB_HEADS * SB_HEAD_DIM
MEM_HEADS = 4
N_GROUPS = 4
EXPERTS_PER_GROUP = 8
N_EXPERTS = N_GROUPS * EXPERTS_PER_GROUP
TOP_K = 2
DEPTH = 1
ALPHA = (2 * DEPTH) ** 0.25
LN_EPS = 1e-5
LOG2E = 1.4426950408889634
RMS_EPS = 1e-6

LANES = 128
SUBLANES = 8
VMEM_LIMIT = 56 * 1024 * 1024

BF16 = jnp.bfloat16
F32 = jnp.float32


def _split_bf16(a):
    hi = a.astype(BF16)
    lo = (a - hi.astype(F32)).astype(BF16)
    return hi, lo


def _dot(a, b):
    return jnp.dot(a, b, preferred_element_type=F32)


def _dot_nt(a, b):
    return lax.dot_general(a, b, (((1,), (1,)), ((), ())), preferred_element_type=F32)


def _layer_norm(r, g, b):
    mu = jnp.mean(r, axis=-1, keepdims=True)
    c = r - mu
    var = jnp.mean(c * c, axis=-1, keepdims=True)
    return c * lax.rsqrt(var + LN_EPS) * g + b


def _proj_conv_kernel(x_ref, w_ref, cw_ref, gn_ref, gmat_ref,
                      yc_ref, q_ref, k_ref, v_ref, ubuf_ref, *, ts):
    s = pl.program_id(1)
    xb = x_ref[0].astype(BF16)
    cw = CONV_WIDTH

    @pl.when(s == 0)
    def _():
        ubuf_ref[0:SUBLANES, :] = jnp.zeros((SUBLANES, cw), F32)

    cc = _dot(xb, w_ref[:, cw:2 * cw])
    ch = _dot(xb, w_ref[:, 2 * cw:3 * cw])
    u = cc * ch
    ubuf_ref[SUBLANES:SUBLANES + ts, :] = u
    um1 = ubuf_ref[SUBLANES - 1:SUBLANES - 1 + ts, :]
    um2 = ubuf_ref[SUBLANES - 2:SUBLANES - 2 + ts, :]
    conv = cw_ref[2:3, :] * u + cw_ref[1:2, :] * um1 + cw_ref[0:1, :] * um2
    ubuf_ref[0:SUBLANES, :] = ubuf_ref[ts:ts + SUBLANES, :]

    cb = _dot(xb, w_ref[:, 0:cw])
    y = cb * conv
    hi, lo = _split_bf16(y * y)
    gsum = _dot(hi, gmat_ref[...]) + _dot(lo, gmat_ref[...])
    yn = y * lax.rsqrt(gsum * (1.0 / CONV_GROUP) + RMS_EPS) * gn_ref[...]
    yc_ref[0] = yn.astype(yc_ref.dtype)

    sw = SB_WIDTH
    base = 3 * cw
    q_ref[0] = (_dot(xb, w_ref[:, base:base + sw]) * (SB_HEAD_DIM ** -0.5 * LOG2E)).astype(q_ref.dtype)
    k_ref[0] = _dot(xb, w_ref[:, base + sw:base + 2 * sw]).astype(k_ref.dtype)
    v_ref[0] = _dot(xb, w_ref[:, base + 2 * sw:base + 3 * sw]).astype(v_ref.dtype)


def _proj_conv(x, w_in_b, conv_w, gn_conv, ts):
    bn, sl, d = x.shape
    gidx = np.arange(CONV_WIDTH) // CONV_GROUP
    gmat = jnp.asarray(gidx[:, None] == gidx[None, :], BF16)
    outs = [jax.ShapeDtypeStruct((bn, sl, CONV_WIDTH), BF16)] * 4
    tile = lambda w: pl.BlockSpec((1, ts, w), lambda b, s: (b, s, 0))
    full = lambda a: pl.BlockSpec(a.shape, lambda b, s: (0,) * a.ndim)
    gn = gn_conv.reshape(1, CONV_WIDTH)
    return pl.pallas_call(
        functools.partial(_proj_conv_kernel, ts=ts),
        grid=(bn, sl // ts),
        in_specs=[tile(d), full(w_in_b), full(conv_w), full(gn), full(gmat)],
        out_specs=[tile(CONV_WIDTH)] * 4,
        out_shape=outs,
        scratch_shapes=[pltpu.VMEM((ts + 2 * SUBLANES, CONV_WIDTH), F32)],
        compiler_params=pltpu.CompilerParams(
            dimension_semantics=("arbitrary", "arbitrary"), vmem_limit_bytes=VMEM_LIMIT),
        name="proj_conv",
    )(x, w_in_b, conv_w, gn, gmat)


SB_TK = 128
SB_UNROLL = 2
SB_ROW_GROUPS = 2
SB_PAIRS = 4
SB_DEAD = 192.0
BF16_BITS_OF_F32 = 0xFFFF0000


def _bits(a):
    return lax.bitcast_convert_type(a, jnp.uint32)


def _softplus2(z2):
    pos = jnp.maximum(z2, 0.0)
    return pos + jnp.log2(1.0 + jnp.exp2(z2 - 2.0 * pos))


def _sb_kernel(q_ref, k_ref, v_ref, gn_ref, cum_ref, o_ref, acc_ref, run_ref, *, tq):
    qi = pl.program_id(2)
    tk = SB_TK
    nd = tq // tk
    lane = lax.broadcasted_iota(jnp.int32, (1, LANES), 1)
    head0 = lane < SB_HEAD_DIM
    zero = jnp.zeros((), BF16)
    lanes_of = lambda p: slice(p * LANES, (p + 1) * LANES)

    def per_head_rows(ref, p, kstart):
        blk = ref[0, pl.ds(kstart, tk), lanes_of(p)]
        return jnp.concatenate([jnp.where(head0, blk, zero), jnp.where(head0, zero, blk)], axis=0)

    def tile(p, row0, row1, kstart, diag, runs):
        rows = row1 - row0
        z2 = _dot_nt(q_ref[0, row0:row1, lanes_of(p)], per_head_rows(k_ref, p, kstart))
        if diag:
            t_loc = lax.broadcasted_iota(jnp.int32, (tk, tk), 0)
            s_loc = lax.broadcasted_iota(jnp.int32, (tk, tk), 1)
            before = s_loc < t_loc
            causal = lambda m: jnp.concatenate(
                [jnp.where(before, m[:tk], 0.0)] + ([m[tk:]] if rows > tk else []), axis=0)
        a_parts, new_runs = [], []
        for h in range(2):
            zh = z2[:, h * tk:(h + 1) * tk]
            sp = _softplus2(zh)
            if diag:
                sp = causal(sp)
            hi = lax.bitcast_convert_type(_bits(sp) & jnp.uint32(BF16_BITS_OF_F32), F32)
            lhs = jnp.concatenate([hi.astype(BF16), (sp - hi).astype(BF16)], axis=1)
            cs = _dot(lhs, cum_ref[...])
            a = jnp.exp2(zh - cs[:, :tk] - runs[h])
            if diag:
                a = causal(a)
            a_parts.append(a.astype(BF16))
            new_runs.append(runs[h] + cs[:, tk:])
        return jnp.concatenate(a_parts, axis=1), new_runs

    acc = [None] * SB_PAIRS
    runs = [None] * SB_PAIRS
    for d in reversed(range(nd)):
        row0 = d * tk
        kstart = pl.multiple_of(qi * tq + row0, tk)
        fresh = jnp.zeros((tk, LANES), F32)
        for p in range(SB_PAIRS):
            grown = ([fresh, fresh] if runs[p] is None
                     else [jnp.concatenate([fresh, r], axis=0) for r in runs[p]])
            a, runs[p] = tile(p, row0, tq, kstart, True, grown)
            av = _dot(a, per_head_rows(v_ref, p, kstart))
            acc[p] = av if acc[p] is None else av + jnp.concatenate([fresh, acc[p]], axis=0)
    for p in range(SB_PAIRS):
        acc_ref[p] = acc[p]
        run_ref[p, 0] = runs[p][0]
        run_ref[p, 1] = runs[p][1]

    n_iter = qi * nd // SB_UNROLL

    def all_dead(runs):
        lowest = functools.reduce(jnp.minimum, [r for pair in runs for r in pair])
        return (jnp.min(lowest) > SB_DEAD).astype(jnp.int32)

    group = tq // SB_ROW_GROUPS
    for g in range(SB_ROW_GROUPS):
        r0, r1 = g * group, (g + 1) * group
        load_runs = lambda r0=r0, r1=r1: [[run_ref[p, h, r0:r1, :] for h in range(2)]
                                          for p in range(SB_PAIRS)]

        def more(carry):
            j, dead = carry
            return jnp.logical_and(j < n_iter, dead == 0)

        def body(carry, r0=r0, r1=r1, load_runs=load_runs):
            j, _ = carry
            runs = load_runs()
            a_parts = [[] for _ in range(SB_PAIRS)]
            v_parts = [[] for _ in range(SB_PAIRS)]
            for u in range(SB_UNROLL):
                kstart = pl.multiple_of((qi * nd - 1 - (j * SB_UNROLL + u)) * tk, tk)
                for p in range(SB_PAIRS):
                    a, runs[p] = tile(p, r0, r1, kstart, False, runs[p])
                    a_parts[p].append(a)
                    v_parts[p].append(per_head_rows(v_ref, p, kstart))
            for p in range(SB_PAIRS):
                acc_ref[p, r0:r1, :] += _dot(jnp.concatenate(a_parts[p], axis=1),
                                             jnp.concatenate(v_parts[p], axis=0))
                run_ref[p, 0, r0:r1, :] = runs[p][0]
                run_ref[p, 1, r0:r1, :] = runs[p][1]
            return j + 1, all_dead(runs)

        lax.while_loop(more, body, (jnp.int32(0), all_dead(load_runs())))

    for p in range(SB_PAIRS):
        y = acc_ref[p]
        y2 = y * y
        s0 = jnp.sum(jnp.where(head0, y2, 0.0), axis=-1, keepdims=True)
        s1 = jnp.sum(jnp.where(head0, 0.0, y2), axis=-1, keepdims=True)
        ms = jnp.where(head0, s0, s1) * (1.0 / SB_HEAD_DIM)
        o_ref[0, :, lanes_of(p)] = (y * lax.rsqrt(ms + RMS_EPS) * gn_ref[:, lanes_of(p)]).astype(o_ref.dtype)


def _sb_attention(q, k, v, gn_sb, tq):
    bn, sl, w = q.shape
    wide = SB_PAIRS * LANES
    j = np.arange(SB_TK)
    assert (tq // SB_TK) % SB_UNROLL == 0 and w % wide == 0
    cum = np.concatenate([(j[:, None] >= j[None, :]), np.ones((SB_TK, SB_TK), bool)], axis=1)
    cum = jnp.asarray(np.concatenate([cum, cum], axis=0), BF16)
    gn = gn_sb.reshape(1, w)
    return pl.pallas_call(
        functools.partial(_sb_kernel, tq=tq),
        grid=(bn, w // wide, sl // tq),
        in_specs=[
            pl.BlockSpec((1, tq, wide), lambda b, p, i: (b, i, p)),
            pl.BlockSpec((1, sl, wide), lambda b, p, i: (b, 0, p)),
            pl.BlockSpec((1, sl, wide), lambda b, p, i: (b, 0, p)),
            pl.BlockSpec((1, wide), lambda b, p, i: (0, p)),
            pl.BlockSpec(cum.shape, lambda b, p, i: (0, 0)),
        ],
        out_specs=pl.BlockSpec((1, tq, wide), lambda b, p, i: (b, i, p)),
        out_shape=jax.ShapeDtypeStruct((bn, sl, w), BF16),
        scratch_shapes=[pltpu.VMEM((SB_PAIRS, tq, LANES), F32), pltpu.VMEM((SB_PAIRS, 2, tq, LANES), F32)],
        compiler_params=pltpu.CompilerParams(
            dimension_semantics=("arbitrary", "arbitrary", "arbitrary"),
            vmem_limit_bytes=VMEM_LIMIT),
        name="sb_attention",
    )(q, k, v, gn, cum)


def _kv_kernel(mem_ref, wk_ref, wv_ref, k_ref, v_ref):
    nb, ml, d = mem_ref.shape
    m = mem_ref[...].reshape(nb * ml, d).astype(BF16)
    k_ref[...] = _dot(m, wk_ref[...]).reshape(nb, ml, d).astype(k_ref.dtype)
    v_ref[...] = _dot(m, wv_ref[...]).reshape(nb, ml, d).astype(v_ref.dtype)


def _kv_proj(mem, wk_b, wv_b, nb):
    bn, ml, d = mem.shape
    blk = pl.BlockSpec((nb, ml, d), lambda i: (i, 0, 0))
    full = pl.BlockSpec((d, d), lambda i: (0, 0))
    return pl.pallas_call(
        _kv_kernel,
        grid=(bn // nb,),
        in_specs=[blk, full, full],
        out_specs=[blk, blk],
        out_shape=[jax.ShapeDtypeStruct((bn, ml, d), BF16)] * 2,
        compiler_params=pltpu.CompilerParams(
            dimension_semantics=("arbitrary",), vmem_limit_bytes=VMEM_LIMIT),
        name="kv_proj",
    )(mem, wk_b, wv_b)


ROUTE_ROWS = 8
ROUTER_ROWS = 40
POST_ROW_GROUPS = 2


def _to_token_tiles(ref, val, ts):
    for c in range(val.shape[1] // LANES):
        ref[pl.ds(c, ts, stride=SUBLANES), :] = val[:, c * LANES:(c + 1) * LANES]


def _from_token_tiles(ref, ts, d):
    return jnp.concatenate(
        [ref[pl.ds(c, ts, stride=SUBLANES), :] for c in range(d // LANES)], axis=1)


def _post_kernel(x_ref, yc_ref, ysb_ref, kb_ref, vb_ref, wo_ref, wq_ref, wmo_ref,
                 ln_ref, wr_ref, br_ref, tri_ref, lt_ref,
                 x2_ref, route_ref, cnt_ref, *, ts):
    d = x_ref.shape[2]
    cw = yc_ref.shape[2]
    group = ts // POST_ROW_GROUPS
    parts = [slice(g * group, (g + 1) * group) for g in range(POST_ROW_GROUPS)]
    h1 = [_dot(yc_ref[0, r, :], wo_ref[0:cw, :]) + _dot(ysb_ref[0, r, :], wo_ref[cw:, :]) for r in parts]
    x1 = [_layer_norm(ALPHA * x_ref[0, r, :] + h, ln_ref[0:1, :], ln_ref[1:2, :]) for r, h in zip(parts, h1)]

    hd = d // MEM_HEADS
    q = [(_dot(a.astype(BF16), wq_ref[...]) * (hd ** -0.5)).astype(BF16) for a in x1]
    outs = [[] for _ in parts]
    for h in range(MEM_HEADS):
        cols = slice(h * hd, (h + 1) * hd)
        for g in range(POST_ROW_GROUPS):
            sc = _dot_nt(q[g][:, cols], kb_ref[0, :, cols])
            p = jnp.exp(sc - jnp.max(sc, axis=-1, keepdims=True))
            den = jnp.sum(p, axis=-1, keepdims=True)
            o = _dot(p.astype(BF16), vb_ref[0, :, cols])
            outs[g].append((o * (1.0 / den)).astype(BF16))
    h2 = [_dot(jnp.concatenate(o, axis=1), wmo_ref[...]) for o in outs]
    x2 = jnp.concatenate([_layer_norm(ALPHA * a + h, ln_ref[2:3, :], ln_ref[3:4, :])
                          for a, h in zip(x1, h2)], axis=0)
    _to_token_tiles(x2_ref, x2, ts)

    lt = _dot_nt(wr_ref[...], x2.astype(BF16)) + br_ref[:, 0:1]
    gl = [lt[g:g + 1, :] for g in range(N_GROUPS)]
    gmax = functools.reduce(jnp.maximum, gl)
    gden = functools.reduce(lambda a, b: a + b, [jnp.exp(l - gmax) for l in gl])
    gprob = 1.0 / gden
    grp = jnp.full(gmax.shape, N_GROUPS - 1, jnp.int32)
    for g in reversed(range(N_GROUPS - 1)):
        grp = jnp.where(gl[g] >= gmax, g, grp)
    el = lt[N_GROUPS:N_GROUPS + EXPERTS_PER_GROUP, :]
    for g in range(1, N_GROUPS):
        lo = N_GROUPS + g * EXPERTS_PER_GROUP
        el = jnp.where(grp == g, lt[lo:lo + EXPERTS_PER_GROUP, :], el)
    row = lax.broadcasted_iota(jnp.int32, el.shape, 0)
    t1 = jnp.max(el, axis=0, keepdims=True)
    i1 = jnp.min(jnp.where(el >= t1, row, EXPERTS_PER_GROUP), axis=0, keepdims=True)
    el2 = jnp.where(row == i1, -jnp.inf, el)
    t2 = jnp.max(el2, axis=0, keepdims=True)
    i2 = jnp.min(jnp.where(el2 >= t2, row, EXPERTS_PER_GROUP), axis=0, keepdims=True)
    e2w = jnp.exp(t2 - t1)
    g0 = gprob / (1.0 + e2w)
    g1 = gprob * e2w / (1.0 + e2w)
    e0 = grp * EXPERTS_PER_GROUP + i1
    e1 = grp * EXPERTS_PER_GROUP + i2

    erow = lax.broadcasted_iota(jnp.int32, (N_EXPERTS, ts), 0)
    oh0 = erow == e0
    oh1 = erow == e1
    both = jnp.where(jnp.logical_or(oh0, oh1), 1.0, 0.0).astype(BF16)
    tot = _dot(both, tri_ref[...])
    lower = _dot(lt_ref[...], both).astype(BF16)
    offs = _dot(lower, tri_ref[:, ts:])
    pos = tot[:, :ts] + offs[:, 0:1]
    p0 = jnp.sum(jnp.where(oh0, pos, 0.0), axis=0, keepdims=True)
    p1 = jnp.sum(jnp.where(oh1, pos, 0.0), axis=0, keepdims=True)
    cnt_ref[0] = tot[:, ts:]

    zero = jnp.zeros_like(g0)
    route_ref[0] = jnp.concatenate(
        [e0.astype(F32), e1.astype(F32), p0, p1, g0, g1, zero, zero], axis=0)


def _post_mixer(x, yc, ysb, kb, vb, wo_b, wq_b, wmo_b, ln, wr_t, br, ts):
    bn, sl, d = x.shape
    cw = yc.shape[2]
    ml = kb.shape[1]
    nt = sl // ts
    j = np.arange(ts)
    tri = jnp.asarray(np.concatenate([j[:, None] < j[None, :], np.ones((ts, LANES), bool)], axis=1), BF16)
    je = np.arange(N_EXPERTS)
    lt = jnp.asarray(je[None, :] < je[:, None], BF16)
    tile = lambda w: pl.BlockSpec((1, ts, w), lambda b, s: (b, s, 0))
    full = lambda a: pl.BlockSpec(a.shape, lambda b, s: (0,) * a.ndim)
    kvspec = pl.BlockSpec((1, ml, d), lambda b, s: (b, 0, 0))
    return pl.pallas_call(
        functools.partial(_post_kernel, ts=ts),
        grid=(bn, nt),
        in_specs=[tile(d), tile(cw), tile(ysb.shape[2]), kvspec, kvspec,
                  full(wo_b), full(wq_b), full(wmo_b), full(ln), full(wr_t), full(br), full(tri),
                  full(lt)],
        out_specs=[
            pl.BlockSpec((ts * SUBLANES, LANES), lambda b, s: (b * nt + s, 0)),
            pl.BlockSpec((1, ROUTE_ROWS, ts), lambda b, s: (b * nt + s, 0, 0)),
            pl.BlockSpec((1, N_EXPERTS, LANES), lambda b, s: (b * nt + s, 0, 0)),
        ],
        out_shape=[
            jax.ShapeDtypeStruct((bn * sl * SUBLANES, LANES), F32),
            jax.ShapeDtypeStruct((bn * nt, ROUTE_ROWS, ts), F32),
            jax.ShapeDtypeStruct((bn * nt, N_EXPERTS, LANES), F32),
        ],
        compiler_params=pltpu.CompilerParams(
            dimension_semantics=("arbitrary", "arbitrary"), vmem_limit_bytes=VMEM_LIMIT),
        name="post_mixer",
    )(x, yc, ysb, kb, vb, wo_b, wq_b, wmo_b, ln, wr_t, br, tri, lt)


def _router_params(ln1_g, ln1_b, ln2_g, ln2_b, w_rg, b_rg, w_re, b_re):
    d = w_rg.shape[0]
    ln = jnp.stack([ln1_g, ln1_b, ln2_g, ln2_b]).astype(F32)
    pad = ROUTER_ROWS - N_GROUPS - N_EXPERTS
    wr_t = jnp.concatenate([w_rg.T, w_re.T, jnp.zeros((pad, d), w_rg.dtype)], axis=0).astype(BF16)
    br = jnp.concatenate([b_rg, b_re, jnp.zeros((pad,), b_rg.dtype)]).astype(F32)
    br = jnp.broadcast_to(br[:, None], (ROUTER_ROWS, LANES))
    return ln, wr_t, br


def _start_copies(cnt_ref, win, list_ref, hbm_ref, buf_ref, sem_of, to_hbm, nbits):
    for b in range(nbits):
        n = (1 << b) * SUBLANES

        def one(i, c, b=b, n=n):
            loc_tok = list_ref[0, 0, b * 2 * N_EXPERTS + i]
            hbm_tok = list_ref[0, 0, (b * 2 + 1) * N_EXPERTS + i]
            loc = buf_ref.at[pl.ds(pl.multiple_of(loc_tok * SUBLANES, SUBLANES), n), :]
            glob = hbm_ref.at[pl.ds(pl.multiple_of(hbm_tok * SUBLANES, SUBLANES), n), :]
            if to_hbm:
                pltpu.make_async_copy(loc, glob, sem_of(b)).start()
            else:
                pltpu.make_async_copy(glob, loc, sem_of(b)).start()
            return c

        lax.fori_loop(0, cnt_ref[win, b], one, 0)


def _wait_copies(cnt_ref, win, hbm_ref, buf_ref, sem_of, to_hbm, nbits):
    for b in range(nbits):
        n = (1 << b) * SUBLANES

        def one(i, c, b=b, n=n):
            loc = buf_ref.at[0:n, :]
            glob = hbm_ref.at[0:n, :]
            if to_hbm:
                pltpu.make_async_copy(loc, glob, sem_of(b)).wait()
            else:
                pltpu.make_async_copy(glob, loc, sem_of(b)).wait()
            return c

        lax.fori_loop(0, cnt_ref[win, b], one, 0)


TOKEN_UNROLL = 8


def _dispatch_kernel(cnt_ref, nu_ref, list_ref, pad_list_ref, pos_ref, x2_ref, xs_ref, sbuf_ref, sems,
                     *, td, tm, nbits):
    w = pl.program_id(0)
    last = pl.num_programs(0) - 1
    slot = w % 2
    scur_ref = sbuf_ref.at[slot]

    @pl.when(w == 0)
    def _():
        block = tm * SUBLANES
        zeros_ref = sbuf_ref.at[1]
        zeros_ref[0:block, :] = jnp.zeros((block, LANES), F32)
        pad_row = pl.num_programs(0)
        sem_of = lambda b: sems.at[1, b]
        _start_copies(cnt_ref, pad_row, pad_list_ref, xs_ref, zeros_ref, sem_of, True, nbits)
        _wait_copies(cnt_ref, pad_row, xs_ref, zeros_ref, sem_of, True, nbits)
        tail = lambda i: pltpu.make_async_copy(
            zeros_ref.at[0:block, :], xs_ref.at[pl.ds(pl.multiple_of(i * block, block), block), :],
            sems.at[1, 0])
        n_blocks = xs_ref.shape[0] // block
        lax.fori_loop(nu_ref[0], n_blocks, lambda i, c: (tail(i).start(), c)[1], 0)
        lax.fori_loop(nu_ref[0], n_blocks, lambda i, c: (tail(i).wait(), c)[1], 0)

    def place(i, c):
        for u in range(TOKEN_UNROLL):
            t = i * TOKEN_UNROLL + u
            tok = x2_ref[pl.ds(pl.multiple_of(t * SUBLANES, SUBLANES), SUBLANES), :]
            for k in range(TOP_K):
                p = pos_ref[0, 0, k * td + t]
                scur_ref[pl.ds(pl.multiple_of(p * SUBLANES, SUBLANES), SUBLANES), :] = tok
        return c

    lax.fori_loop(0, td // TOKEN_UNROLL, place, 0)
    _start_copies(cnt_ref, w, list_ref, xs_ref, scur_ref, lambda b: sems.at[slot, b], True, nbits)

    @pl.when(w > 0)
    def _():
        _wait_copies(cnt_ref, w - 1, xs_ref, sbuf_ref.at[1 - slot], lambda b: sems.at[1 - slot, b],
                     True, nbits)

    @pl.when(w == last)
    def _():
        _wait_copies(cnt_ref, w, xs_ref, scur_ref, lambda b: sems.at[slot, b], True, nbits)


def _dispatch(counts, n_used, lists, pos, x2t, n_rows, td, tm):
    nt = pos.shape[0]
    nbits = counts.shape[1]
    assert tm <= TOP_K * td and n_rows % tm == 0 and nbits == td.bit_length()
    list_len = lists.shape[2]
    return pl.pallas_call(
        functools.partial(_dispatch_kernel, td=td, tm=tm, nbits=nbits),
        grid_spec=pltpu.PrefetchScalarGridSpec(
            num_scalar_prefetch=2,
            grid=(nt,),
            in_specs=[
                pl.BlockSpec((1, 1, list_len), lambda i, c, nu: (i, 0, 0), memory_space=pltpu.SMEM),
                pl.BlockSpec((1, 1, list_len), lambda i, c, nu: (nt, 0, 0), memory_space=pltpu.SMEM),
                pl.BlockSpec((1, 1, TOP_K * td), lambda i, c, nu: (i, 0, 0), memory_space=pltpu.SMEM),
                pl.BlockSpec((td * SUBLANES, LANES), lambda i, c, nu: (i, 0)),
            ],
            out_specs=pl.BlockSpec(memory_space=pl.ANY),
            scratch_shapes=[pltpu.VMEM((2, TOP_K * td * SUBLANES, LANES), F32),
                            pltpu.SemaphoreType.DMA((2, nbits))],
        ),
        out_shape=jax.ShapeDtypeStruct((n_rows * SUBLANES, LANES), F32),
        compiler_params=pltpu.CompilerParams(
            dimension_semantics=("arbitrary",), vmem_limit_bytes=VMEM_LIMIT),
        name="dispatch",
    )(counts, n_used, lists, lists, pos, x2t)


def _expert_kernel(be_ref, nu_ref, xs_ref, wg_ref, wu_ref, wd_ref, ys_ref, *, tm):
    del be_ref
    d = wg_ref.shape[1]

    @pl.when(pl.program_id(0) < nu_ref[0])
    def _():
        xb = _from_token_tiles(xs_ref, tm, d).astype(BF16)
        g = _dot(xb, wg_ref[0].astype(BF16))
        u = _dot(xb, wu_ref[0].astype(BF16))
        h = g * jax.nn.sigmoid(g) * u
        _to_token_tiles(ys_ref, _dot(h.astype(BF16), wd_ref[0].astype(BF16)), tm)

    @pl.when(pl.program_id(0) >= nu_ref[0])
    def _():
        ys_ref[...] = jnp.zeros(ys_ref.shape, ys_ref.dtype)


def _experts(block_expert, n_used, xs, wg, wu, wd, tm):
    ne, d, f = wg.shape
    nb = xs.shape[0] // (tm * SUBLANES)
    rows = lambda i, be, nu: (i, 0)
    wsel = lambda i, be, nu: (be[i], 0, 0)
    return pl.pallas_call(
        functools.partial(_expert_kernel, tm=tm),
        grid_spec=pltpu.PrefetchScalarGridSpec(
            num_scalar_prefetch=2,
            grid=(nb,),
            in_specs=[
                pl.BlockSpec((tm * SUBLANES, LANES), rows),
                pl.BlockSpec((1, d, f), wsel),
                pl.BlockSpec((1, d, f), wsel),
                pl.BlockSpec((1, f, d), wsel),
            ],
            out_specs=pl.BlockSpec((tm * SUBLANES, LANES), rows),
        ),
        out_shape=jax.ShapeDtypeStruct(xs.shape, F32),
        compiler_params=pltpu.CompilerParams(
            dimension_semantics=("arbitrary",), vmem_limit_bytes=VMEM_LIMIT),
        name="experts",
    )(block_expert, n_used, xs, wg, wu, wd)


def _combine_kernel(cnt_ref, list_ref, next_list_ref, pos_ref, gate_ref, x2_ref, ln_ref, ys_ref, o_ref,
                    ybuf_ref, rbuf_ref, sems, *, tc, nbits):
    w = pl.program_id(0)
    d = o_ref.shape[1]
    slot = w % 2
    ycur_ref = ybuf_ref.at[slot]

    @pl.when(w == 0)
    def _():
        _start_copies(cnt_ref, w, list_ref, ys_ref, ycur_ref, lambda b: sems.at[slot, b], False, nbits)

    @pl.when(w + 1 < pl.num_programs(0))
    def _():
        _start_copies(cnt_ref, w + 1, next_list_ref, ys_ref, ybuf_ref.at[1 - slot],
                      lambda b: sems.at[1 - slot, b], False, nbits)

    _wait_copies(cnt_ref, w, ys_ref, ycur_ref, lambda b: sems.at[slot, b], False, nbits)

    def gather(i, c):
        for u in range(TOKEN_UNROLL):
            t = i * TOKEN_UNROLL + u
            rows = pl.ds(pl.multiple_of(t * SUBLANES, SUBLANES), SUBLANES)
            moe = None
            for k in range(TOP_K):
                p = pos_ref[0, 0, k * tc + t]
                y = ycur_ref[pl.ds(pl.multiple_of(p * SUBLANES, SUBLANES), SUBLANES), :]
                term = gate_ref[0, 0, k * tc + t] * y
                moe = term if moe is None else moe + term
            rbuf_ref[rows, :] = ALPHA * x2_ref[rows, :] + moe
        return c

    lax.fori_loop(0, tc // TOKEN_UNROLL, gather, 0)
    o_ref[...] = _layer_norm(_from_token_tiles(rbuf_ref, tc, d), ln_ref[0:1, :], ln_ref[1:2, :])


def _combine(counts, lists, pos, gates, x2t, ln3, ys, tc):
    nt = pos.shape[0]
    d = ln3.shape[1]
    nbits = counts.shape[1]
    list_len = lists.shape[2]
    smem_tile = pl.BlockSpec((1, 1, TOP_K * tc), lambda i, c: (i, 0, 0), memory_space=pltpu.SMEM)
    return pl.pallas_call(
        functools.partial(_combine_kernel, tc=tc, nbits=nbits),
        grid_spec=pltpu.PrefetchScalarGridSpec(
            num_scalar_prefetch=1,
            grid=(nt,),
            in_specs=[
                pl.BlockSpec((1, 1, list_len), lambda i, c: (i, 0, 0), memory_space=pltpu.SMEM),
                pl.BlockSpec((1, 1, list_len), lambda i, c: (jnp.minimum(i + 1, nt - 1), 0, 0),
                             memory_space=pltpu.SMEM),
                smem_tile,
                smem_tile,
                pl.BlockSpec((tc * SUBLANES, LANES), lambda i, c: (i, 0)),
                pl.BlockSpec(ln3.shape, lambda i, c: (0, 0)),
                pl.BlockSpec(memory_space=pl.ANY),
            ],
            out_specs=pl.BlockSpec((tc, d), lambda i, c: (i, 0)),
            scratch_shapes=[pltpu.VMEM((2, TOP_K * tc * SUBLANES, LANES), F32),
                            pltpu.VMEM((tc * SUBLANES, LANES), F32),
                            pltpu.SemaphoreType.DMA((2, nbits))],
        ),
        out_shape=jax.ShapeDtypeStruct((nt * tc, d), F32),
        compiler_params=pltpu.CompilerParams(
            dimension_semantics=("arbitrary",), vmem_limit_bytes=VMEM_LIMIT),
        name="combine",
    )(counts, lists, lists, pos, gates, x2t, ln3, ys)


def _routing_plan(route, wcnt, tm):
    nt, _, ts = route.shape
    nbits = ts.bit_length()
    cw = wcnt[:, :, 0].astype(jnp.int32)
    cnt = jnp.sum(cw, axis=0)
    padded = (cnt + tm - 1) // tm * tm
    pend = jnp.cumsum(padded)
    pstart = pend - padded
    base = jnp.cumsum(cw, axis=0) - cw
    offs = jnp.cumsum(cw, axis=1) - cw
    c_all = jnp.concatenate([cw, (padded - cnt)[None, :]], axis=0)
    o_all = jnp.concatenate([offs, jnp.zeros_like(cnt)[None, :]], axis=0)
    d_all = jnp.concatenate([pstart[None, :] + base, (pstart + cnt)[None, :]], axis=0)
    j = jnp.arange(N_EXPERTS, dtype=jnp.int32)
    per_class, counts = [], []
    for b in range(nbits):
        has = (c_all >> b) & 1
        done = (c_all >> (b + 1)) << (b + 1)
        slot = jnp.cumsum(has, axis=1) - has
        pick = ((slot[:, :, None] == j[None, None, :]) & (has[:, :, None] == 1)).astype(jnp.int32)
        per_class.append(jnp.einsum('wej,we->wj', pick, o_all + done))
        per_class.append(jnp.einsum('wej,we->wj', pick, d_all + done))
        counts.append(jnp.sum(has, axis=1))
    lists = jnp.concatenate(per_class, axis=1).reshape(nt + 1, 1, nbits * 2 * N_EXPERTS)
    counts = jnp.stack(counts, axis=1)
    pos = route[:, 2:4, :].astype(jnp.int32).reshape(nt, 1, TOP_K * ts)
    gates = route[:, 4:6, :].reshape(nt, 1, TOP_K * ts)
    nb = nt * ts * TOP_K // tm + N_EXPERTS
    first_row = jnp.arange(nb, dtype=jnp.int32) * tm
    block_expert = jnp.minimum(
        jnp.sum((pend[None, :] <= first_row[:, None]).astype(jnp.int32), axis=1), N_EXPERTS - 1)
    n_used = (pend[-1:] // tm).astype(jnp.int32)
    return lists, counts, pos, gates, block_expert, n_used, nb


def _moe(x2t, route, wcnt, wg, wu, wd, ln3, tm):
    ts = route.shape[2]
    lists, counts, pos, gates, block_expert, n_used, nb = _routing_plan(route, wcnt, tm)
    xs = _dispatch(counts, n_used, lists, pos, x2t, nb * tm, ts, tm)
    ys = _experts(block_expert, n_used, xs, wg, wu, wd, tm)
    return _combine(counts, lists, pos, gates, x2t, ln3, ys, ts)


def kernel(x, mem, w_in, conv_w, gn_conv, gn_sb, w_out, ln1_g, ln1_b, w_mq, w_mk, w_mv, w_mo,
           ln2_g, ln2_b, w_router_group, b_router_group, w_router_expert, b_router_expert,
           w_gate, w_up, w_down, ln3_g, ln3_b):
    for l in range(DEPTH):
        x = _layer(x, mem, w_in[l], conv_w[l], gn_conv[l], gn_sb[l], w_out[l], ln1_g[l], ln1_b[l],
                   w_mq[l], w_mk[l], w_mv[l], w_mo[l], ln2_g[l], ln2_b[l],
                   w_router_group[l], b_router_group[l], w_router_expert[l], b_router_expert[l],
                   w_gate[l], w_up[l], w_down[l], ln3_g[l], ln3_b[l])
    return x


def _layer(x, mem, w_in, conv_w, gn_conv, gn_sb, w_out, ln1_g, ln1_b, w_mq, w_mk, w_mv, w_mo,
           ln2_g, ln2_b, w_rg, b_rg, w_re, b_re, w_gate, w_up, w_down, ln3_g, ln3_b,
           ts=512, tq=512, tm=512, kv_nb=4):
    bn, sl, d = x.shape
    bf = lambda w: w.astype(BF16)
    yc, q, k, v = _proj_conv(x, bf(w_in), conv_w, gn_conv, ts)
    ysb = _sb_attention(q, k, v, gn_sb, tq)
    kb, vb = _kv_proj(mem, bf(w_mk), bf(w_mv), kv_nb)
    ln, wr_t, br = _router_params(ln1_g, ln1_b, ln2_g, ln2_b, w_rg, b_rg, w_re, b_re)
    x2t, route, wcnt = _post_mixer(x, yc, ysb, kb, vb, bf(w_out), bf(w_mq), bf(w_mo),
                                   ln, wr_t, br, ts)
    ln3 = jnp.stack([ln3_g, ln3_b]).astype(F32)
    out = _moe(x2t, route, wcnt, w_gate, w_up, w_down, ln3, tm)
    return out.reshape(bn, sl, d)
```

```python
import functools

import jax
import jax.numpy as jnp
import numpy as np
from jax import lax
from jax.experimental import pallas as pl
from jax.experimental.pallas import tpu as pltpu

CONV_WIDTH = 512
CONV_GROUP = 64
CONV_K = 3
SB_HEADS = 8
SB_HEAD_DIM = 64
SB_WIDTH = SB_HEADS * SB_HEAD_DIM
MEM_HEADS = 4
N_GROUPS = 4
EXPERTS_PER_GROUP = 8
N_EXPERTS = N_GROUPS * EXPERTS_PER_GROUP
TOP_K = 2
DEPTH = 1
ALPHA = (2 * DEPTH) ** 0.25
LN_EPS = 1e-5
LOG2E = 1.4426950408889634
RMS_EPS = 1e-6

LANES = 128
SUBLANES = 8
VMEM_LIMIT = 56 * 1024 * 1024

BF16 = jnp.bfloat16
F32 = jnp.float32


def _split_bf16(a):
    hi = a.astype(BF16)
    lo = (a - hi.astype(F32)).astype(BF16)
    return hi, lo


def _dot(a, b):
    return jnp.dot(a, b, preferred_element_type=F32)


def _dot_nt(a, b):
    return lax.dot_general(a, b, (((1,), (1,)), ((), ())), preferred_element_type=F32)


def _layer_norm(r, g, b):
    mu = jnp.mean(r, axis=-1, keepdims=True)
    c = r - mu
    var = jnp.mean(c * c, axis=-1, keepdims=True)
    return c * lax.rsqrt(var + LN_EPS) * g + b


def _proj_conv_kernel(x_ref, w_ref, cw_ref, gn_ref, gmat_ref,
                      yc_ref, q_ref, k_ref, v_ref, ubuf_ref, *, ts):
    s = pl.program_id(1)
    xb = x_ref[0].astype(BF16)
    cw = CONV_WIDTH

    @pl.when(s == 0)
    def _():
        ubuf_ref[0:SUBLANES, :] = jnp.zeros((SUBLANES, cw), F32)

    cc = _dot(xb, w_ref[:, cw:2 * cw])
    ch = _dot(xb, w_ref[:, 2 * cw:3 * cw])
    u = cc * ch
    ubuf_ref[SUBLANES:SUBLANES + ts, :] = u
    um1 = ubuf_ref[SUBLANES - 1:SUBLANES - 1 + ts, :]
    um2 = ubuf_ref[SUBLANES - 2:SUBLANES - 2 + ts, :]
    conv = cw_ref[2:3, :] * u + cw_ref[1:2, :] * um1 + cw_ref[0:1, :] * um2
    ubuf_ref[0:SUBLANES, :] = ubuf_ref[ts:ts + SUBLANES, :]

    cb = _dot(xb, w_ref[:, 0:cw])
    y = cb * conv
    hi, lo = _split_bf16(y * y)
    gsum = _dot(hi, gmat_ref[...]) + _dot(lo, gmat_ref[...])
    yn = y * lax.rsqrt(gsum * (1.0 / CONV_GROUP) + RMS_EPS) * gn_ref[...]
    yc_ref[0] = yn.astype(yc_ref.dtype)

    sw = SB_WIDTH
    base = 3 * cw
    q_ref[0] = (_dot(xb, w_ref[:, base:base + sw]) * (SB_HEAD_DIM ** -0.5 * LOG2E)).astype(q_ref.dtype)
    k_ref[0] = _dot(xb, w_ref[:, base + sw:base + 2 * sw]).astype(k_ref.dtype)
    v_ref[0] = _dot(xb, w_ref[:, base + 2 * sw:base + 3 * sw]).astype(v_ref.dtype)


def _proj_conv(x, w_in_b, conv_w, gn_conv, ts):
    bn, sl, d = x.shape
    gidx = np.arange(CONV_WIDTH) // CONV_GROUP
    gmat = jnp.asarray(gidx[:, None] == gidx[None, :], BF16)
    outs = [jax.ShapeDtypeStruct((bn, sl, CONV_WIDTH), BF16)] * 4
    tile = lambda w: pl.BlockSpec((1, ts, w), lambda b, s: (b, s, 0))
    full = lambda a: pl.BlockSpec(a.shape, lambda b, s: (0,) * a.ndim)
    gn = gn_conv.reshape(1, CONV_WIDTH)
    return pl.pallas_call(
        functools.partial(_proj_conv_kernel, ts=ts),
        grid=(bn, sl // ts),
        in_specs=[tile(d), full(w_in_b), full(conv_w), full(gn), full(gmat)],
        out_specs=[tile(CONV_WIDTH)] * 4,
        out_shape=outs,
        scratch_shapes=[pltpu.VMEM((ts + 2 * SUBLANES, CONV_WIDTH), F32)],
        compiler_params=pltpu.CompilerParams(
            dimension_semantics=("arbitrary", "arbitrary"), vmem_limit_bytes=VMEM_LIMIT),
        name="proj_conv",
    )(x, w_in_b, conv_w, gn, gmat)


SB_TK = 128
SB_UNROLL = 2
SB_ROW_GROUPS = 2
SB_PAIRS = 4
SB_DEAD = 192.0
BF16_BITS_OF_F32 = 0xFFFF0000


def _bits(a):
    return lax.bitcast_convert_type(a, jnp.uint32)


def _softplus2(z2):
    pos = jnp.maximum(z2, 0.0)
    return pos + jnp.log2(1.0 + jnp.exp2(z2 - 2.0 * pos))


def _sb_kernel(q_ref, k_ref, v_ref, gn_ref, cum_ref, o_ref, acc_ref, run_ref, *, tq):
    qi = pl.program_id(2)
    tk = SB_TK
    nd = tq // tk
    lane = lax.broadcasted_iota(jnp.int32, (1, LANES), 1)
    head0 = lane < SB_HEAD_DIM
    zero = jnp.zeros((), BF16)
    lanes_of = lambda p: slice(p * LANES, (p + 1) * LANES)

    def per_head_rows(ref, p, kstart):
        blk = ref[0, pl.ds(kstart, tk), lanes_of(p)]
        return jnp.concatenate([jnp.where(head0, blk, zero), jnp.where(head0, zero, blk)], axis=0)

    def tile(p, row0, row1, kstart, diag, runs):
        rows = row1 - row0
        z2 = _dot_nt(q_ref[0, row0:row1, lanes_of(p)], per_head_rows(k_ref, p, kstart))
        if diag:
            t_loc = lax.broadcasted_iota(jnp.int32, (tk, tk), 0)
            s_loc = lax.broadcasted_iota(jnp.int32, (tk, tk), 1)
            before = s_loc < t_loc
            causal = lambda m: jnp.concatenate(
                [jnp.where(before, m[:tk], 0.0)] + ([m[tk:]] if rows > tk else []), axis=0)
        a_parts, new_runs = [], []
        for h in range(2):
            zh = z2[:, h * tk:(h + 1) * tk]
            sp = _softplus2(zh)
            if diag:
                sp = causal(sp)
            hi = lax.bitcast_convert_type(_bits(sp) & jnp.uint32(BF16_BITS_OF_F32), F32)
            lhs = jnp.concatenate([hi.astype(BF16), (sp - hi).astype(BF16)], axis=1)
            cs = _dot(lhs, cum_ref[...])
            a = jnp.exp2(zh - cs[:, :tk] - runs[h])
            if diag:
                a = causal(a)
            a_parts.append(a.astype(BF16))
            new_runs.append(runs[h] + cs[:, tk:])
        return jnp.concatenate(a_parts, axis=1), new_runs

    acc = [None] * SB_PAIRS
    runs = [None] * SB_PAIRS
    for d in reversed(range(nd)):
        row0 = d * tk
        kstart = pl.multiple_of(qi * tq + row0, tk)
        fresh = jnp.zeros((tk, LANES), F32)
        for p in range(SB_PAIRS):
            grown = ([fresh, fresh] if runs[p] is None
                     else [jnp.concatenate([fresh, r], axis=0) for r in runs[p]])
            a, runs[p] = tile(p, row0, tq, kstart, True, grown)
            av = _dot(a, per_head_rows(v_ref, p, kstart))
            acc[p] = av if acc[p] is None else av + jnp.concatenate([fresh, acc[p]], axis=0)
    for p in range(SB_PAIRS):
        acc_ref[p] = acc[p]
        run_ref[p, 0] = runs[p][0]
        run_ref[p, 1] = runs[p][1]

    n_iter = qi * nd // SB_UNROLL

    def all_dead(runs):
        lowest = functools.reduce(jnp.minimum, [r for pair in runs for r in pair])
        return (jnp.min(lowest) > SB_DEAD).astype(jnp.int32)

    group = tq // SB_ROW_GROUPS
    for g in range(SB_ROW_GROUPS):
        r0, r1 = g * group, (g + 1) * group
        load_runs = lambda r0=r0, r1=r1: [[run_ref[p, h, r0:r1, :] for h in range(2)]
                                          for p in range(SB_PAIRS)]

        def more(carry):
            j, dead = carry
            return jnp.logical_and(j < n_iter, dead == 0)

        def body(carry, r0=r0, r1=r1, load_runs=load_runs):
            j, _ = carry
            runs = load_runs()
            a_parts = [[] for _ in range(SB_PAIRS)]
            v_parts = [[] for _ in range(SB_PAIRS)]
            for u in range(SB_UNROLL):
                kstart = pl.multiple_of((qi * nd - 1 - (j * SB_UNROLL + u)) * tk, tk)
                for p in range(SB_PAIRS):
                    a, runs[p] = tile(p, r0, r1, kstart, False, runs[p])
                    a_parts[p].append(a)
                    v_parts[p].append(per_head_rows(v_ref, p, kstart))
            for p in range(SB_PAIRS):
                acc_ref[p, r0:r1, :] += _dot(jnp.concatenate(a_parts[p], axis=1),
                                             jnp.concatenate(v_parts[p], axis=0))
                run_ref[p, 0, r0:r1, :] = runs[p][0]
                run_ref[p, 1, r0:r1, :] = runs[p][1]
            return j + 1, all_dead(runs)

        lax.while_loop(more, body, (jnp.int32(0), all_dead(load_runs())))

    for p in range(SB_PAIRS):
        y = acc_ref[p]
        y2 = y * y
        s0 = jnp.sum(jnp.where(head0, y2, 0.0), axis=-1, keepdims=True)
        s1 = jnp.sum(jnp.where(head0, 0.0, y2), axis=-1, keepdims=True)
        ms = jnp.where(head0, s0, s1) * (1.0 / SB_HEAD_DIM)
        o_ref[0, :, lanes_of(p)] = (y * lax.rsqrt(ms + RMS_EPS) * gn_ref[:, lanes_of(p)]).astype(o_ref.dtype)


def _sb_attention(q, k, v, gn_sb, tq):
    bn, sl, w = q.shape
    wide = SB_PAIRS * LANES
    j = np.arange(SB_TK)
    assert (tq // SB_TK) % SB_UNROLL == 0 and w % wide == 0
    cum = np.concatenate([(j[:, None] >= j[None, :]), np.ones((SB_TK, SB_TK), bool)], axis=1)
    cum = jnp.asarray(np.concatenate([cum, cum], axis=0), BF16)
    gn = gn_sb.reshape(1, w)
    return pl.pallas_call(
        functools.partial(_sb_kernel, tq=tq),
        grid=(bn, w // wide, sl // tq),
        in_specs=[
            pl.BlockSpec((1, tq, wide), lambda b, p, i: (b, i, p)),
            pl.BlockSpec((1, sl, wide), lambda b, p, i: (b, 0, p)),
            pl.BlockSpec((1, sl, wide), lambda b, p, i: (b, 0, p)),
            pl.BlockSpec((1, wide), lambda b, p, i: (0, p)),
            pl.BlockSpec(cum.shape, lambda b, p, i: (0, 0)),
        ],
        out_specs=pl.BlockSpec((1, tq, wide), lambda b, p, i: (b, i, p)),
        out_shape=jax.ShapeDtypeStruct((bn, sl, w), BF16),
        scratch_shapes=[pltpu.VMEM((SB_PAIRS, tq, LANES), F32), pltpu.VMEM((SB_PAIRS, 2, tq, LANES), F32)],
        compiler_params=pltpu.CompilerParams(
            dimension_semantics=("arbitrary", "arbitrary", "arbitrary"),
            vmem_limit_bytes=VMEM_LIMIT),
        name="sb_attention",
    )(q, k, v, gn, cum)


def _kv_kernel(mem_ref, wk_ref, wv_ref, k_ref, v_ref):
    nb, ml, d = mem_ref.shape
    m = mem_ref[...].reshape(nb * ml, d).astype(BF16)
    k_ref[...] = _dot(m, wk_ref[...]).reshape(nb, ml, d).astype(k_ref.dtype)
    v_ref[...] = _dot(m, wv_ref[...]).reshape(nb, ml, d).astype(v_ref.dtype)


def _kv_proj(mem, wk_b, wv_b, nb):
    bn, ml, d = mem.shape
    blk = pl.BlockSpec((nb, ml, d), lambda i: (i, 0, 0))
    full = pl.BlockSpec((d, d), lambda i: (0, 0))
    return pl.pallas_call(
        _kv_kernel,
        grid=(bn // nb,),
        in_specs=[blk, full, full],
        out_specs=[blk, blk],
        out_shape=[jax.ShapeDtypeStruct((bn, ml, d), BF16)] * 2,
        compiler_params=pltpu.CompilerParams(
            dimension_semantics=("arbitrary",), vmem_limit_bytes=VMEM_LIMIT),
        name="kv_proj",
    )(mem, wk_b, wv_b)


ROUTE_ROWS = 8
ROUTER_ROWS = 40
POST_ROW_GROUPS = 2


def _to_token_tiles(ref, val, ts):
    for c in range(val.shape[1] // LANES):
        ref[pl.ds(c, ts, stride=SUBLANES), :] = val[:, c * LANES:(c + 1) * LANES]


def _from_token_tiles(ref, ts, d):
    return jnp.concatenate(
        [ref[pl.ds(c, ts, stride=SUBLANES), :] for c in range(d // LANES)], axis=1)


def _post_kernel(x_ref, yc_ref, ysb_ref, kb_ref, vb_ref, wo_ref, wq_ref, wmo_ref,
                 ln_ref, wr_ref, br_ref, tri_ref, lt_ref,
                 x2_ref, route_ref, cnt_ref, *, ts):
    d = x_ref.shape[2]
    cw = yc_ref.shape[2]
    group = ts // POST_ROW_GROUPS
    parts = [slice(g * group, (g + 1) * group) for g in range(POST_ROW_GROUPS)]
    h1 = [_dot(yc_ref[0, r, :], wo_ref[0:cw, :]) + _dot(ysb_ref[0, r, :], wo_ref[cw:, :]) for r in parts]
    x1 = [_layer_norm(ALPHA * x_ref[0, r, :] + h, ln_ref[0:1, :], ln_ref[1:2, :]) for r, h in zip(parts, h1)]

    hd = d // MEM_HEADS
    q = [(_dot(a.astype(BF16), wq_ref[...]) * (hd ** -0.5)).astype(BF16) for a in x1]
    outs = [[] for _ in parts]
    for h in range(MEM_HEADS):
        cols = slice(h * hd, (h + 1) * hd)
        for g in range(POST_ROW_GROUPS):
            sc = _dot_nt(q[g][:, cols], kb_ref[0, :, cols])
            p = jnp.exp(sc - jnp.max(sc, axis=-1, keepdims=True))
            den = jnp.sum(p, axis=-1, keepdims=True)
            o = _dot(p.astype(BF16), vb_ref[0, :, cols])
            outs[g].append((o * (1.0 / den)).astype(BF16))
    h2 = [_dot(jnp.concatenate(o, axis=1), wmo_ref[...]) for o in outs]
    x2 = jnp.concatenate([_layer_norm(ALPHA * a + h, ln_ref[2:3, :], ln_ref[3:4, :])
                          for a, h in zip(x1, h2)], axis=0)
    _to_token_tiles(x2_ref, x2, ts)

    lt = _dot_nt(wr_ref[...], x2.astype(BF16)) + br_ref[:, 0:1]
    gl = [lt[g:g + 1, :] for g in range(N_GROUPS)]
    gmax = functools.reduce(jnp.maximum, gl)
    gden = functools.reduce(lambda a, b: a + b, [jnp.exp(l - gmax) for l in gl])
    gprob = 1.0 / gden
    grp = jnp.full(gmax.shape, N_GROUPS - 1, jnp.int32)
    for g in reversed(range(N_GROUPS - 1)):
        grp = jnp.where(gl[g] >= gmax, g, grp)
    el = lt[N_GROUPS:N_GROUPS + EXPERTS_PER_GROUP, :]
    for g in range(1, N_GROUPS):
        lo = N_GROUPS + g * EXPERTS_PER_GROUP
        el = jnp.where(grp == g, lt[lo:lo + EXPERTS_PER_GROUP, :], el)
    row = lax.broadcasted_iota(jnp.int32, el.shape, 0)
    t1 = jnp.max(el, axis=0, keepdims=True)
    i1 = jnp.min(jnp.where(el >= t1, row, EXPERTS_PER_GROUP), axis=0, keepdims=True)
    el2 = jnp.where(row == i1, -jnp.inf, el)
    t2 = jnp.max(el2, axis=0, keepdims=True)
    i2 = jnp.min(jnp.where(el2 >= t2, row, EXPERTS_PER_GROUP), axis=0, keepdims=True)
    e2w = jnp.exp(t2 - t1)
    g0 = gprob / (1.0 + e2w)
    g1 = gprob * e2w / (1.0 + e2w)
    e0 = grp * EXPERTS_PER_GROUP + i1
    e1 = grp * EXPERTS_PER_GROUP + i2

    erow = lax.broadcasted_iota(jnp.int32, (N_EXPERTS, ts), 0)
    oh0 = erow == e0
    oh1 = erow == e1
    both = jnp.where(jnp.logical_or(oh0, oh1), 1.0, 0.0).astype(BF16)
    tot = _dot(both, tri_ref[...])
    lower = _dot(lt_ref[...], both).astype(BF16)
    offs = _dot(lower, tri_ref[:, ts:])
    pos = tot[:, :ts] + offs[:, 0:1]
    p0 = jnp.sum(jnp.where(oh0, pos, 0.0), axis=0, keepdims=True)
    p1 = jnp.sum(jnp.where(oh1, pos, 0.0), axis=0, keepdims=True)
    cnt_ref[0] = tot[:, ts:]

    zero = jnp.zeros_like(g0)
    route_ref[0] = jnp.concatenate(
        [e0.astype(F32), e1.astype(F32), p0, p1, g0, g1, zero, zero], axis=0)


def _post_mixer(x, yc, ysb, kb, vb, wo_b, wq_b, wmo_b, ln, wr_t, br, ts):
    bn, sl, d = x.shape
    cw = yc.shape[2]
    ml = kb.shape[1]
    nt = sl // ts
    j = np.arange(ts)
    tri = jnp.asarray(np.concatenate([j[:, None] < j[None, :], np.ones((ts, LANES), bool)], axis=1), BF16)
    je = np.arange(N_EXPERTS)
    lt = jnp.asarray(je[None, :] < je[:, None], BF16)
    tile = lambda w: pl.BlockSpec((1, ts, w), lambda b, s: (b, s, 0))
    full = lambda a: pl.BlockSpec(a.shape, lambda b, s: (0,) * a.ndim)
    kvspec = pl.BlockSpec((1, ml, d), lambda b, s: (b, 0, 0))
    return pl.pallas_call(
        functools.partial(_post_kernel, ts=ts),
        grid=(bn, nt),
        in_specs=[tile(d), tile(cw), tile(ysb.shape[2]), kvspec, kvspec,
                  full(wo_b), full(wq_b), full(wmo_b), full(ln), full(wr_t), full(br), full(tri),
                  full(lt)],
        out_specs=[
            pl.BlockSpec((ts * SUBLANES, LANES), lambda b, s: (b * nt + s, 0)),
            pl.BlockSpec((1, ROUTE_ROWS, ts), lambda b, s: (b * nt + s, 0, 0)),
            pl.BlockSpec((1, N_EXPERTS, LANES), lambda b, s: (b * nt + s, 0, 0)),
        ],
        out_shape=[
            jax.ShapeDtypeStruct((bn * sl * SUBLANES, LANES), F32),
            jax.ShapeDtypeStruct((bn * nt, ROUTE_ROWS, ts), F32),
            jax.ShapeDtypeStruct((bn * nt, N_EXPERTS, LANES), F32),
        ],
        compiler_params=pltpu.CompilerParams(
            dimension_semantics=("arbitrary", "arbitrary"), vmem_limit_bytes=VMEM_LIMIT),
        name="post_mixer",
    )(x, yc, ysb, kb, vb, wo_b, wq_b, wmo_b, ln, wr_t, br, tri, lt)


def _router_params(ln1_g, ln1_b, ln2_g, ln2_b, w_rg, b_rg, w_re, b_re):
    d = w_rg.shape[0]
    ln = jnp.stack([ln1_g, ln1_b, ln2_g, ln2_b]).astype(F32)
    pad = ROUTER_ROWS - N_GROUPS - N_EXPERTS
    wr_t = jnp.concatenate([w_rg.T, w_re.T, jnp.zeros((pad, d), w_rg.dtype)], axis=0).astype(BF16)
    br = jnp.concatenate([b_rg, b_re, jnp.zeros((pad,), b_rg.dtype)]).astype(F32)
    br = jnp.broadcast_to(br[:, None], (ROUTER_ROWS, LANES))
    return ln, wr_t, br


def _start_copies(cnt_ref, win, list_ref, hbm_ref, buf_ref, sem_of, to_hbm, nbits):
    for b in range(nbits):
        n = (1 << b) * SUBLANES

        def one(i, c, b=b, n=n):
            loc_tok = list_ref[0, 0, b * 2 * N_EXPERTS + i]
            hbm_tok = list_ref[0, 0, (b * 2 + 1) * N_EXPERTS + i]
            loc = buf_ref.at[pl.ds(pl.multiple_of(loc_tok * SUBLANES, SUBLANES), n), :]
            glob = hbm_ref.at[pl.ds(pl.multiple_of(hbm_tok * SUBLANES, SUBLANES), n), :]
            if to_hbm:
                pltpu.make_async_copy(loc, glob, sem_of(b)).start(priority=b % 2)
            else:
                pltpu.make_async_copy(glob, loc, sem_of(b)).start(priority=b % 2)
            return c

        lax.fori_loop(0, cnt_ref[win, b], one, 0)


def _wait_copies(cnt_ref, win, hbm_ref, buf_ref, sem_of, to_hbm, nbits):
    for b in range(nbits):
        n = (1 << b) * SUBLANES

        def one(i, c, b=b, n=n):
            loc = buf_ref.at[0:n, :]
            glob = hbm_ref.at[0:n, :]
            if to_hbm:
                pltpu.make_async_copy(loc, glob, sem_of(b)).wait()
            else:
                pltpu.make_async_copy(glob, loc, sem_of(b)).wait()
            return c

        lax.fori_loop(0, cnt_ref[win, b], one, 0)


TOKEN_UNROLL = 8


def _dispatch_kernel(cnt_ref, nu_ref, list_ref, pad_list_ref, pos_ref, x2_ref, xs_ref, sbuf_ref, sems,
                     *, td, tm, nbits):
    w = pl.program_id(0)
    last = pl.num_programs(0) - 1
    slot = w % 2
    scur_ref = sbuf_ref.at[slot]

    @pl.when(w == 0)
    def _():
        block = tm * SUBLANES
        zeros_ref = sbuf_ref.at[1]
        zeros_ref[0:block, :] = jnp.zeros((block, LANES), F32)
        pad_row = pl.num_programs(0)
        sem_of = lambda b: sems.at[1, b]
        _start_copies(cnt_ref, pad_row, pad_list_ref, xs_ref, zeros_ref, sem_of, True, nbits)
        _wait_copies(cnt_ref, pad_row, xs_ref, zeros_ref, sem_of, True, nbits)
        tail = lambda i: pltpu.make_async_copy(
            zeros_ref.at[0:block, :], xs_ref.at[pl.ds(pl.multiple_of(i * block, block), block), :],
            sems.at[1, 0])
        n_blocks = xs_ref.shape[0] // block
        lax.fori_loop(nu_ref[0], n_blocks, lambda i, c: (tail(i).start(), c)[1], 0)
        lax.fori_loop(nu_ref[0], n_blocks, lambda i, c: (tail(i).wait(), c)[1], 0)

    def place(i, c):
        for u in range(TOKEN_UNROLL):
            t = i * TOKEN_UNROLL + u
            tok = x2_ref[pl.ds(pl.multiple_of(t * SUBLANES, SUBLANES), SUBLANES), :]
            for k in range(TOP_K):
                p = pos_ref[0, 0, k * td + t]
                scur_ref[pl.ds(pl.multiple_of(p * SUBLANES, SUBLANES), SUBLANES), :] = tok
        return c

    lax.fori_loop(0, td // TOKEN_UNROLL, place, 0)
    _start_copies(cnt_ref, w, list_ref, xs_ref, scur_ref, lambda b: sems.at[slot, b], True, nbits)

    @pl.when(w > 0)
    def _():
        _wait_copies(cnt_ref, w - 1, xs_ref, sbuf_ref.at[1 - slot], lambda b: sems.at[1 - slot, b],
                     True, nbits)

    @pl.when(w == last)
    def _():
        _wait_copies(cnt_ref, w, xs_ref, scur_ref, lambda b: sems.at[slot, b], True, nbits)


def _dispatch(counts, n_used, lists, pos, x2t, n_rows, td, tm):
    nt = pos.shape[0]
    nbits = counts.shape[1]
    assert tm <= TOP_K * td and n_rows % tm == 0 and nbits == td.bit_length()
    list_len = lists.shape[2]
    return pl.pallas_call(
        functools.partial(_dispatch_kernel, td=td, tm=tm, nbits=nbits),
        grid_spec=pltpu.PrefetchScalarGridSpec(
            num_scalar_prefetch=2,
            grid=(nt,),
            in_specs=[
                pl.BlockSpec((1, 1, list_len), lambda i, c, nu: (i, 0, 0), memory_space=pltpu.SMEM),
                pl.BlockSpec((1, 1, list_len), lambda i, c, nu: (nt, 0, 0), memory_space=pltpu.SMEM),
                pl.BlockSpec((1, 1, TOP_K * td), lambda i, c, nu: (i, 0, 0), memory_space=pltpu.SMEM),
                pl.BlockSpec((td * SUBLANES, LANES), lambda i, c, nu: (i, 0)),
            ],
            out_specs=pl.BlockSpec(memory_space=pl.ANY),
            scratch_shapes=[pltpu.VMEM((2, TOP_K * td * SUBLANES, LANES), F32),
                            pltpu.SemaphoreType.DMA((2, nbits))],
        ),
        out_shape=jax.ShapeDtypeStruct((n_rows * SUBLANES, LANES), F32),
        compiler_params=pltpu.CompilerParams(
            dimension_semantics=("arbitrary",), vmem_limit_bytes=VMEM_LIMIT),
        name="dispatch",
    )(counts, n_used, lists, lists, pos, x2t)


def _expert_kernel(be_ref, nu_ref, xs_ref, wg_ref, wu_ref, wd_ref, ys_ref, *, tm):
    del be_ref
    d = wg_ref.shape[1]

    @pl.when(pl.program_id(0) < nu_ref[0])
    def _():
        xb = _from_token_tiles(xs_ref, tm, d).astype(BF16)
        g = _dot(xb, wg_ref[0].astype(BF16))
        u = _dot(xb, wu_ref[0].astype(BF16))
        h = g * jax.nn.sigmoid(g) * u
        _to_token_tiles(ys_ref, _dot(h.astype(BF16), wd_ref[0].astype(BF16)), tm)

    @pl.when(pl.program_id(0) >= nu_ref[0])
    def _():
        ys_ref[...] = jnp.zeros(ys_ref.shape, ys_ref.dtype)


def _experts(block_expert, n_used, xs, wg, wu, wd, tm):
    ne, d, f = wg.shape
    nb = xs.shape[0] // (tm * SUBLANES)
    rows = lambda i, be, nu: (i, 0)
    wsel = lambda i, be, nu: (be[i], 0, 0)
    return pl.pallas_call(
        functools.partial(_expert_kernel, tm=tm),
        grid_spec=pltpu.PrefetchScalarGridSpec(
            num_scalar_prefetch=2,
            grid=(nb,),
            in_specs=[
                pl.BlockSpec((tm * SUBLANES, LANES), rows),
                pl.BlockSpec((1, d, f), wsel),
                pl.BlockSpec((1, d, f), wsel),
                pl.BlockSpec((1, f, d), wsel),
            ],
            out_specs=pl.BlockSpec((tm * SUBLANES, LANES), rows),
        ),
        out_shape=jax.ShapeDtypeStruct(xs.shape, F32),
        compiler_params=pltpu.CompilerParams(
            dimension_semantics=("arbitrary",), vmem_limit_bytes=VMEM_LIMIT),
        name="experts",
    )(block_expert, n_used, xs, wg, wu, wd)


def _combine_kernel(cnt_ref, list_ref, next_list_ref, pos_ref, gate_ref, x2_ref, ln_ref, ys_ref, o_ref,
                    ybuf_ref, rbuf_ref, sems, *, tc, nbits):
    w = pl.program_id(0)
    d = o_ref.shape[1]
    slot = w % 2
    ycur_ref = ybuf_ref.at[slot]

    @pl.when(w == 0)
    def _():
        _start_copies(cnt_ref, w, list_ref, ys_ref, ycur_ref, lambda b: sems.at[slot, b], False, nbits)

    @pl.when(w + 1 < pl.num_programs(0))
    def _():
        _start_copies(cnt_ref, w + 1, next_list_ref, ys_ref, ybuf_ref.at[1 - slot],
                      lambda b: sems.at[1 - slot, b], False, nbits)

    _wait_copies(cnt_ref, w, ys_ref, ycur_ref, lambda b: sems.at[slot, b], False, nbits)

    def gather(i, c):
        for u in range(TOKEN_UNROLL):
            t = i * TOKEN_UNROLL + u
            rows = pl.ds(pl.multiple_of(t * SUBLANES, SUBLANES), SUBLANES)
            moe = None
            for k in range(TOP_K):
                p = pos_ref[0, 0, k * tc + t]
                y = ycur_ref[pl.ds(pl.multiple_of(p * SUBLANES, SUBLANES), SUBLANES), :]
                term = gate_ref[0, 0, k * tc + t] * y
                moe = term if moe is None else moe + term
            rbuf_ref[rows, :] = ALPHA * x2_ref[rows, :] + moe
        return c

    lax.fori_loop(0, tc // TOKEN_UNROLL, gather, 0)
    o_ref[...] = _layer_norm(_from_token_tiles(rbuf_ref, tc, d), ln_ref[0:1, :], ln_ref[1:2, :])


def _combine(counts, lists, pos, gates, x2t, ln3, ys, tc):
    nt = pos.shape[0]
    d = ln3.shape[1]
    nbits = counts.shape[1]
    list_len = lists.shape[2]
    smem_tile = pl.BlockSpec((1, 1, TOP_K * tc), lambda i, c: (i, 0, 0), memory_space=pltpu.SMEM)
    return pl.pallas_call(
        functools.partial(_combine_kernel, tc=tc, nbits=nbits),
        grid_spec=pltpu.PrefetchScalarGridSpec(
            num_scalar_prefetch=1,
            grid=(nt,),
            in_specs=[
                pl.BlockSpec((1, 1, list_len), lambda i, c: (i, 0, 0), memory_space=pltpu.SMEM),
                pl.BlockSpec((1, 1, list_len), lambda i, c: (jnp.minimum(i + 1, nt - 1), 0, 0),
                             memory_space=pltpu.SMEM),
                smem_tile,
                smem_tile,
                pl.BlockSpec((tc * SUBLANES, LANES), lambda i, c: (i, 0)),
                pl.BlockSpec(ln3.shape, lambda i, c: (0, 0)),
                pl.BlockSpec(memory_space=pl.ANY),
            ],
            out_specs=pl.BlockSpec((tc, d), lambda i, c: (i, 0)),
            scratch_shapes=[pltpu.VMEM((2, TOP_K * tc * SUBLANES, LANES), F32),
                            pltpu.VMEM((tc * SUBLANES, LANES), F32),
                            pltpu.SemaphoreType.DMA((2, nbits))],
        ),
        out_shape=jax.ShapeDtypeStruct((nt * tc, d), F32),
        compiler_params=pltpu.CompilerParams(
            dimension_semantics=("arbitrary",), vmem_limit_bytes=VMEM_LIMIT),
        name="combine",
    )(counts, lists, lists, pos, gates, x2t, ln3, ys)


def _routing_plan(route, wcnt, tm):
    nt, _, ts = route.shape
    nbits = ts.bit_length()
    cw = wcnt[:, :, 0].astype(jnp.int32)
    cnt = jnp.sum(cw, axis=0)
    padded = (cnt + tm - 1) // tm * tm
    pend = jnp.cumsum(padded)
    pstart = pend - padded
    base = jnp.cumsum(cw, axis=0) - cw
    offs = jnp.cumsum(cw, axis=1) - cw
    c_all = jnp.concatenate([cw, (padded - cnt)[None, :]], axis=0)
    o_all = jnp.concatenate([offs, jnp.zeros_like(cnt)[None, :]], axis=0)
    d_all = jnp.concatenate([pstart[None, :] + base, (pstart + cnt)[None, :]], axis=0)
    j = jnp.arange(N_EXPERTS, dtype=jnp.int32)
    per_class, counts = [], []
    for b in range(nbits):
        has = (c_all >> b) & 1
        done = (c_all >> (b + 1)) << (b + 1)
        slot = jnp.cumsum(has, axis=1) - has
        pick = ((slot[:, :, None] == j[None, None, :]) & (has[:, :, None] == 1)).astype(jnp.int32)
        per_class.append(jnp.einsum('wej,we->wj', pick, o_all + done))
        per_class.append(jnp.einsum('wej,we->wj', pick, d_all + done))
        counts.append(jnp.sum(has, axis=1))
    lists = jnp.concatenate(per_class, axis=1).reshape(nt + 1, 1, nbits * 2 * N_EXPERTS)
    counts = jnp.stack(counts, axis=1)
    pos = route[:, 2:4, :].astype(jnp.int32).reshape(nt, 1, TOP_K * ts)
    gates = route[:, 4:6, :].reshape(nt, 1, TOP_K * ts)
    nb = nt * ts * TOP_K // tm + N_EXPERTS
    first_row = jnp.arange(nb, dtype=jnp.int32) * tm
    block_expert = jnp.minimum(
        jnp.sum((pend[None, :] <= first_row[:, None]).astype(jnp.int32), axis=1), N_EXPERTS - 1)
    n_used = (pend[-1:] // tm).astype(jnp.int32)
    return lists, counts, pos, gates, block_expert, n_used, nb


def _moe(x2t, route, wcnt, wg, wu, wd, ln3, tm):
    ts = route.shape[2]
    lists, counts, pos, gates, block_expert, n_used, nb = _routing_plan(route, wcnt, tm)
    xs = _dispatch(counts, n_used, lists, pos, x2t, nb * tm, ts, tm)
    ys = _experts(block_expert, n_used, xs, wg, wu, wd, tm)
    return _combine(counts, lists, pos, gates, x2t, ln3, ys, ts)


def kernel(x, mem, w_in, conv_w, gn_conv, gn_sb, w_out, ln1_g, ln1_b, w_mq, w_mk, w_mv, w_mo,
           ln2_g, ln2_b, w_router_group, b_router_group, w_router_expert, b_router_expert,
           w_gate, w_up, w_down, ln3_g, ln3_b):
    for l in range(DEPTH):
        x = _layer(x, mem, w_in[l], conv_w[l], gn_conv[l], gn_sb[l], w_out[l], ln1_g[l], ln1_b[l],
                   w_mq[l], w_mk[l], w_mv[l], w_mo[l], ln2_g[l], ln2_b[l],
                   w_router_group[l], b_router_group[l], w_router_expert[l], b_router_expert[l],
                   w_gate[l], w_up[l], w_down[l], ln3_g[l], ln3_b[l])
    return x


def _layer(x, mem, w_in, conv_w, gn_conv, gn_sb, w_out, ln1_g, ln1_b, w_mq, w_mk, w_mv, w_mo,
           ln2_g, ln2_b, w_rg, b_rg, w_re, b_re, w_gate, w_up, w_down, ln3_g, ln3_b,
           ts=512, tq=512, tm=512, kv_nb=4):
    bn, sl, d = x.shape
    bf = lambda w: w.astype(BF16)
    yc, q, k, v = _proj_conv(x, bf(w_in), conv_w, gn_conv, ts)
    ysb = _sb_attention(q, k, v, gn_sb, tq)
    kb, vb = _kv_proj(mem, bf(w_mk), bf(w_mv), kv_nb)
    ln, wr_t, br = _router_params(ln1_g, ln1_b, ln2_g, ln2_b, w_rg, b_rg, w_re, b_re)
    x2t, route, wcnt = _post_mixer(x, yc, ysb, kb, vb, bf(w_out), bf(w_mq), bf(w_mo),
                                   ln, wr_t, br, ts)
    ln3 = jnp.stack([ln3_g, ln3_b]).astype(F32)
    out = _moe(x2t, route, wcnt, w_gate, w_up, w_down, ln3, tm)
    return out.reshape(bn, sl, d)
```
